```python
import math
import jax
import jax.numpy as jnp
from jax import lax
import numpy as np

D_MODEL = 1024
BATCH = 8
SEQ = 4096
DEPTH = 1

CHUNK = 64
EPS = 1e-6

GDN_HEADS = 8
GDN_DK = 128
GDN_DV = 128
GDN_QK_W = GDN_HEADS * GDN_DK
GDN_V_W = GDN_HEADS * GDN_DV
CONV_K = 4

HGRN_HEADS = 8
HGRN_DK = 128
HGRN_DV = 128
HGRN_K_W = HGRN_HEADS * HGRN_DK
HGRN_V_W = HGRN_HEADS * HGRN_DV

N_MEM = 256
XA_HEADS = 4
XA_DH = D_MODEL // XA_HEADS

N_GROUPS = 4
EXP_PER_GROUP = 8
N_EXPERTS = N_GROUPS * EXP_PER_GROUP
TOP_K = 2
D_FF_EXPERT = 512
MOE_BLOCK = 256

IN_SPLITS = (2 * GDN_QK_W + GDN_V_W, GDN_HEADS, GDN_HEADS, GDN_V_W,
             HGRN_K_W, HGRN_K_W, HGRN_V_W, HGRN_V_W, D_MODEL, D_MODEL)
IN_COLS = 2 * GDN_QK_W + 2 * GDN_V_W + 2 * GDN_HEADS + 2 * HGRN_K_W + 2 * HGRN_V_W + 2 * D_MODEL

kernel_name = 'hybrid_gdn_hgrn2_xattn_hmoe'


def rmsnorm(x, w):
    xf = x.astype(jnp.float32)
    xf = xf * lax.rsqrt(jnp.mean(xf * xf, axis=-1, keepdims=True) + EPS)
    return (xf * w.astype(jnp.float32)).astype(x.dtype)


def l2norm(x):
    xf = x.astype(jnp.float32)
    return (xf * lax.rsqrt(jnp.sum(xf * xf, axis=-1, keepdims=True) + EPS)).astype(x.dtype)


def split_cols(t, sizes):
    out, start = [], 0
    for n in sizes:
        out.append(t[..., start:start + n])
        start += n
    return out


def causal_depthwise_conv(x, w):
    k, c = w.shape
    return lax.conv_general_dilated(
        x, w[:, None, :].astype(x.dtype), window_strides=(1,), padding=[(k - 1, 0)],
        dimension_numbers=('NWC', 'WIO', 'NWC'), feature_group_count=c)


def to_chunks(t):
    b, s, h, d = t.shape
    return t.reshape(b, s // CHUNK, CHUNK, h, d).transpose(0, 3, 1, 2, 4)


def scalar_chunks(t):
    b, s, h = t.shape
    return t.reshape(b, s // CHUNK, CHUNK, h).transpose(0, 3, 1, 2)


def from_chunks(t):
    b, h, n, c, d = t.shape
    return t.transpose(0, 2, 3, 1, 4).reshape(b, n * c, h, d)


def gated_delta_rule(q, k, v, g, beta):
    c = q.shape[-2]
    dk = q.shape[-1]
    dv = v.shape[-1]
    q = q * dk ** -0.5
    causal = jnp.tril(jnp.ones((c, c), dtype=bool))
    strict = jnp.tril(jnp.ones((c, c), dtype=bool), -1)
    cum = jnp.cumsum(g, axis=-1)
    decay = jnp.exp(jnp.where(causal, cum[..., :, None] - cum[..., None, :], -jnp.inf))
    kb = k * beta[..., None]
    a_low = jnp.where(strict, jnp.einsum('bhnid,bhnjd->bhnij', kb, k) * decay, 0.0)
    eye = jnp.eye(c, dtype=q.dtype)
    rhs = jnp.concatenate([v * beta[..., None], kb * jnp.exp(cum)[..., None]], axis=-1)
    sol = lax.linalg.triangular_solve(eye + a_low, rhs, left_side=True, lower=True)
    u, w = sol[..., :dv], sol[..., dv:]
    qk_intra = jnp.where(causal, jnp.einsum('bhnid,bhnjd->bhnij', q, k) * decay, 0.0)
    q_dec = q * jnp.exp(cum)[..., None]
    last = cum[..., -1:]
    k_dec = k * jnp.exp(last - cum)[..., None]
    chunk_decay = jnp.exp(last[..., 0])

    def step(state, xs):
        qi, wi, ui, ki, ai, di = xs
        v_new = ui - jnp.einsum('bhcd,bhde->bhce', wi, state)
        o = jnp.einsum('bhcd,bhde->bhce', qi, state) + jnp.einsum('bhij,bhje->bhie', ai, v_new)
        state = state * di[..., None, None] + jnp.einsum('bhcd,bhce->bhde', ki, v_new)
        return state, o

    b, h = q.shape[0], q.shape[1]
    s0 = jnp.zeros((b, h, dk, dv), q.dtype)
    xs = tuple(jnp.moveaxis(t, 2, 0) for t in (q_dec, w, u, k_dec, qk_intra, chunk_decay))
    _, o = lax.scan(step, s0, xs)
    return jnp.moveaxis(o, 0, 2)


def hgrn2_chunked(q, k, v, log_f):
    c = q.shape[-2]
    causal = jnp.tril(jnp.ones((c, c), dtype=bool))
    cum = jnp.cumsum(log_f, axis=-2)
    q_in = q * jnp.exp(cum)
    k_in = k * jnp.exp(-cum)
    intra = jnp.where(causal, jnp.einsum('bhnid,bhnjd->bhnij', q_in, k_in), 0.0)
    last = cum[..., -1:, :]
    k_dec = k * jnp.exp(last - cum)
    chunk_decay = jnp.exp(last[..., 0, :])

    def step(state, xs):
        qi, ki, vi, ai, di = xs
        o = jnp.einsum('bhcd,bhde->bhce', qi, state) + jnp.einsum('bhij,bhje->bhie', ai, vi)
        state = di[..., :, None] * state + jnp.einsum('bhcd,bhce->bhde', ki, vi)
        return state, o

    b, h, _, _, dk = q.shape
    dv = v.shape[-1]
    s0 = jnp.zeros((b, h, dk, dv), q.dtype)
    xs = tuple(jnp.moveaxis(t, 2, 0) for t in (q_in, k_dec, v, intra, chunk_decay))
    _, o = lax.scan(step, s0, xs)
    return jnp.moveaxis(o, 0, 2)


def mixer_block(hn, w_in, conv_w, a_log, dt_bias, gdn_norm_w, lb, hgrn_norm_w,
                w_branch_a, w_branch_b, w_out):
    b, s, _ = hn.shape
    dt = hn.dtype
    f32 = jnp.float32
    (qkv_a, alpha_pre, beta_pre, og_a, f_pre, q_b, i_b, og_b, gate_a, gate_b) = split_cols(hn @ w_in, IN_SPLITS)

    qkv = jax.nn.silu(causal_depthwise_conv(qkv_a, conv_w))
    q_a = l2norm(qkv[..., :GDN_QK_W].reshape(b, s, GDN_HEADS, GDN_DK))
    k_a = l2norm(qkv[..., GDN_QK_W:2 * GDN_QK_W].reshape(b, s, GDN_HEADS, GDN_DK))
    v_a = qkv[..., 2 * GDN_QK_W:].reshape(b, s, GDN_HEADS, GDN_DV)
    beta = jax.nn.sigmoid(beta_pre.astype(f32))
    log_alpha = -jnp.exp(a_log.astype(f32)) * jax.nn.softplus(alpha_pre.astype(f32) + dt_bias.astype(f32))
    o_a = gated_delta_rule(to_chunks(q_a).astype(f32), to_chunks(k_a).astype(f32), to_chunks(v_a).astype(f32),
                           scalar_chunks(log_alpha), scalar_chunks(beta))
    o_a = from_chunks(o_a).astype(dt)
    y_a = (rmsnorm(o_a, gdn_norm_w) * jax.nn.silu(og_a.reshape(b, s, GDN_HEADS, GDN_DV))).reshape(b, s, GDN_V_W)

    forget = lb + (1.0 - lb) * jax.nn.sigmoid(f_pre.astype(f32))
    log_f = jnp.log(forget)
    k_b = 1.0 - forget
    q_b = q_b.astype(f32) * HGRN_DK ** -0.5
    o_b = hgrn2_chunked(to_chunks(q_b.reshape(b, s, HGRN_HEADS, HGRN_DK)),
                        to_chunks(k_b.reshape(b, s, HGRN_HEADS, HGRN_DK)),
                        to_chunks(i_b.astype(f32).reshape(b, s, HGRN_HEADS, HGRN_DV)),
                        to_chunks(log_f.reshape(b, s, HGRN_HEADS, HGRN_DK)))
    o_b = from_chunks(o_b).astype(dt)
    y_b = (rmsnorm(o_b, hgrn_norm_w) * jax.nn.silu(og_b.reshape(b, s, HGRN_HEADS, HGRN_DV))).reshape(b, s, HGRN_V_W)

    merged = jax.nn.sigmoid(gate_a) * (y_a @ w_branch_a) + jax.nn.sigmoid(gate_b) * (y_b @ w_branch_b)
    return merged @ w_out


def cross_attention(hn, memn, wq, wkv, wo):
    b, s, _ = hn.shape
    m = memn.shape[1]
    q = (hn @ wq).reshape(b, s, XA_HEADS, XA_DH)
    kv = memn @ wkv
    k = kv[..., :D_MODEL].reshape(b, m, XA_HEADS, XA_DH)
    v = kv[..., D_MODEL:].reshape(b, m, XA_HEADS, XA_DH)
    scores = jnp.einsum('bshd,bmhd->bhsm', q, k).astype(jnp.float32) * XA_DH ** -0.5
    p = jax.nn.softmax(scores, axis=-1).astype(v.dtype)
    o = jnp.einsum('bhsm,bmhd->bshd', p, v).reshape(b, s, D_MODEL)
    return o @ wo


def hier_moe(hn, wrg, brg, wre, bre, w_gate, w_up, w_down):
    b, s, d = hn.shape
    t = hn.reshape(-1, d)
    n_tok = t.shape[0]
    g_prob = jax.nn.softmax((t @ wrg + brg).astype(jnp.float32), axis=-1)
    g_p, g_idx = lax.top_k(g_prob, 1)
    e_logits = (t @ wre + bre).astype(jnp.float32).reshape(n_tok, N_GROUPS, EXP_PER_GROUP)
    e_logits = jnp.take_along_axis(e_logits, g_idx[:, :, None], axis=1)[:, 0]
    e_p, e_idx = lax.top_k(jax.nn.softmax(e_logits, axis=-1), TOP_K)
    weights = g_p * e_p / jnp.sum(e_p, axis=-1, keepdims=True)
    expert = g_idx * EXP_PER_GROUP + e_idx

    m = n_tok * TOP_K
    flat_e = expert.reshape(-1).astype(jnp.int32)
    flat_tok = jnp.repeat(jnp.arange(n_tok, dtype=jnp.int32), TOP_K)
    flat_w = weights.reshape(-1)
    order = jnp.argsort(flat_e)
    se, stok, sw = flat_e[order], flat_tok[order], flat_w[order]
    sizes = jnp.bincount(se, length=N_EXPERTS).astype(jnp.int32)
    padded = ((sizes + MOE_BLOCK - 1) // MOE_BLOCK) * MOE_BLOCK
    starts = jnp.cumsum(sizes) - sizes
    pend = jnp.cumsum(padded)
    pstart = pend - padded
    dest = pstart[se] + (jnp.arange(m, dtype=jnp.int32) - starts[se])
    n_rows = ((m + MOE_BLOCK - 1) // MOE_BLOCK) * MOE_BLOCK + N_EXPERTS * MOE_BLOCK
    n_blocks = n_rows // MOE_BLOCK
    x_pad = jnp.zeros((n_rows, d), t.dtype).at[dest].set(t[stok])
    w_pad = jnp.zeros((n_rows,), jnp.float32).at[dest].set(sw)
    tok_pad = jnp.full((n_rows,), n_tok, jnp.int32).at[dest].set(stok)
    blk_start = jnp.arange(n_blocks, dtype=jnp.int32) * MOE_BLOCK
    blk_e = jnp.clip(jnp.searchsorted(pend, blk_start, side='right'), 0, N_EXPERTS - 1)

    def block_mlp(args):
        xb, e = args
        return (jax.nn.silu(xb @ w_gate[e]) * (xb @ w_up[e])) @ w_down[e]

    y_pad = lax.map(block_mlp, (x_pad.reshape(n_blocks, MOE_BLOCK, d), blk_e)).reshape(n_rows, d)
    y_pad = y_pad * w_pad[:, None].astype(y_pad.dtype)
    out = jax.ops.segment_sum(y_pad, tok_pad, num_segments=n_tok)
    return out.reshape(b, s, d)


def setup_inputs(seed: int = 0) -> dict:
    key = jax.random.key(seed)
    ks = jax.random.split(key, 32)
    f32 = jnp.float32
    L = DEPTH

    def nrm(k, shape, fan_in):
        return jax.random.normal(k, shape, f32) * fan_in ** -0.5

    def gain(k, shape):
        return 1.0 + 0.02 * jax.random.normal(k, shape, f32)

    dt_init = jnp.exp(jax.random.uniform(ks[6], (L, GDN_HEADS), f32, math.log(1e-3), math.log(1e-1)))
    return {
        'x': jax.random.normal(ks[0], (BATCH, SEQ, D_MODEL), f32),
        'mem': jax.random.normal(ks[1], (BATCH, N_MEM, D_MODEL), f32),
        'norm_mix_w': gain(ks[2], (L, D_MODEL)),
        'w_in': nrm(ks[3], (L, D_MODEL, IN_COLS), D_MODEL),
        'conv_w': nrm(ks[4], (L, CONV_K, 2 * GDN_QK_W + GDN_V_W), CONV_K),
        'gdn_a_log': jnp.log(jax.random.uniform(ks[5], (L, GDN_HEADS), f32, 1.0, 16.0)),
        'gdn_dt_bias': jnp.log(jnp.expm1(dt_init)),
        'gdn_out_norm_w': gain(ks[7], (L, GDN_DV)),
        'hgrn_lb': 0.1 * jax.random.normal(ks[8], (L + 1, HGRN_K_W), f32),
        'hgrn_out_norm_w': gain(ks[9], (L, HGRN_DV)),
        'w_branch_a': nrm(ks[10], (L, GDN_V_W, D_MODEL), GDN_V_W),
        'w_branch_b': nrm(ks[11], (L, HGRN_V_W, D_MODEL), HGRN_V_W),
        'w_out': nrm(ks[12], (L, D_MODEL, D_MODEL), D_MODEL),
        'norm_xattn_w': gain(ks[13], (L, D_MODEL)),
        'norm_mem_w': gain(ks[14], (L, D_MODEL)),
        'xattn_wq': nrm(ks[15], (L, D_MODEL, D_MODEL), D_MODEL),
        'xattn_wkv': nrm(ks[16], (L, D_MODEL, 2 * D_MODEL), D_MODEL),
        'xattn_wo': nrm(ks[17], (L, D_MODEL, D_MODEL), D_MODEL),
        'norm_ffn_w': gain(ks[18], (L, D_MODEL)),
        'router_group_w': nrm(ks[19], (L, D_MODEL, N_GROUPS), D_MODEL),
        'router_group_b': 0.01 * jax.random.normal(ks[20], (L, N_GROUPS), f32),
        'router_expert_w': nrm(ks[21], (L, D_MODEL, N_EXPERTS), D_MODEL),
        'router_expert_b': 0.01 * jax.random.normal(ks[22], (L, N_EXPERTS), f32),
        'expert_w_gate': nrm(ks[23], (L, N_EXPERTS, D_MODEL, D_FF_EXPERT), D_MODEL),
        'expert_w_up': nrm(ks[24], (L, N_EXPERTS, D_MODEL, D_FF_EXPERT), D_MODEL),
        'expert_w_down': nrm(ks[25], (L, N_EXPERTS, D_FF_EXPERT, D_MODEL), D_FF_EXPERT),
        'final_norm_w': gain(ks[26], (D_MODEL,)),
    }


def reference(x, mem, norm_mix_w, w_in, conv_w, gdn_a_log, gdn_dt_bias, gdn_out_norm_w,
              hgrn_lb, hgrn_out_norm_w, w_branch_a, w_branch_b, w_out,
              norm_xattn_w, norm_mem_w, xattn_wq, xattn_wkv, xattn_wo,
              norm_ffn_w, router_group_w, router_group_b, router_expert_w, router_expert_b,
              expert_w_gate, expert_w_up, expert_w_down, final_norm_w):
    lb_all = jnp.cumsum(jax.nn.softmax(hgrn_lb.astype(jnp.float32), axis=0), axis=0)
    h = x
    for layer in range(DEPTH):
        hn = rmsnorm(h, norm_mix_w[layer])
        h = h + mixer_block(hn, w_in[layer], conv_w[layer], gdn_a_log[layer], gdn_dt_bias[layer],
                            gdn_out_norm_w[layer], lb_all[layer], hgrn_out_norm_w[layer],
                            w_branch_a[layer], w_branch_b[layer], w_out[layer])
        hn = rmsnorm(h, norm_xattn_w[layer])
        memn = rmsnorm(mem, norm_mem_w[layer])
        h = h + cross_attention(hn, memn, xattn_wq[layer], xattn_wkv[layer], xattn_wo[layer])
        hn = rmsnorm(h, norm_ffn_w[layer])
        h = h + hier_moe(hn, router_group_w[layer], router_group_b[layer], router_expert_w[layer],
                         router_expert_b[layer], expert_w_gate[layer], expert_w_up[layer], expert_w_down[layer])
    return rmsnorm(h, final_norm_w)
```

```python
import functools

import jax
import jax.numpy as jnp
from jax import lax
from jax.experimental import pallas as pl
from jax.experimental.pallas import tpu as pltpu

F32 = jnp.float32
BF16 = jnp.bfloat16
I32 = jnp.int32

EPS = 1e-6
CHUNK = 64
HEADS = 8
HEAD_DIM = 128
CONV_K = 4
XA_HEADS = 4
N_GROUPS = 4
EXP_PER_GROUP = 8
N_EXPERTS = N_GROUPS * EXP_PER_GROUP
MOE_BLOCK = 256
LANES = 128
RANK_BITS = 20

VMEM_LIMIT = 52 * 1024 * 1024


def _cparams(n_axes):
    return pltpu.CompilerParams(dimension_semantics=("arbitrary",) * n_axes,
                                vmem_limit_bytes=VMEM_LIMIT)


def _dot(a, b):
    return jnp.dot(a.astype(BF16), b.astype(BF16), preferred_element_type=F32)


def _dot_nt(a, b):
    return lax.dot_general(a.astype(BF16), b.astype(BF16), (((1,), (1,)), ((), ())),
                           preferred_element_type=F32)


def _dot_tn(a, b):
    return lax.dot_general(a.astype(BF16), b.astype(BF16), (((0,), (0,)), ((), ())),
                           preferred_element_type=F32)


def _split(x):
    hi = x.astype(BF16)
    lo = (x - hi.astype(F32)).astype(BF16)
    return hi, lo


def _dot_exact_lhs(m_bf16, x):
    hi, lo = _split(x)
    return (jnp.dot(m_bf16, hi, preferred_element_type=F32)
            + jnp.dot(m_bf16, lo, preferred_element_type=F32))


def _rms(x):
    return x * lax.rsqrt(jnp.mean(x * x, axis=-1, keepdims=True) + EPS)


def _silu(x):
    return x * jax.nn.sigmoid(x)


def _softplus(x):
    return jnp.maximum(x, 0.0) + jnp.log(1.0 + jnp.exp(-jnp.abs(x)))


def _chunk_masks(tc):
    ri = lax.broadcasted_iota(I32, (tc, tc), 0)
    ci = lax.broadcasted_iota(I32, (tc, tc), 1)
    same = (ri // CHUNK) == (ci // CHUNK)
    causal = same & (ri >= ci)
    strict = same & (ri > ci)
    return ri, ci, causal, strict


def _proj_kernel(x_ref, nw_ref, w_ref, ws_ref, out_ref, small_ref, hn_ref):
    @pl.when(pl.program_id(1) == 0)
    def _():
        hn = (_rms(x_ref[...]) * nw_ref[...]).astype(BF16)
        hn_ref[...] = hn
        small_ref[...] = jnp.dot(hn, ws_ref[...], preferred_element_type=F32)

    out_ref[...] = jnp.dot(hn_ref[...], w_ref[...], preferred_element_type=F32).astype(BF16)


def _proj(x2, nw, w_main, w_small, tm, tn):
    t, d = x2.shape
    n = w_main.shape[1]
    return pl.pallas_call(
        _proj_kernel,
        grid=(t // tm, n // tn),
        in_specs=[
            pl.BlockSpec((tm, d), lambda i, j: (i, 0)),
            pl.BlockSpec((1, d), lambda i, j: (0, 0)),
            pl.BlockSpec((d, tn), lambda i, j: (0, j)),
            pl.BlockSpec((d, LANES), lambda i, j: (0, 0)),
        ],
        out_specs=[
            pl.BlockSpec((tm, tn), lambda i, j: (i, j)),
            pl.BlockSpec((tm, LANES), lambda i, j: (i, 0)),
        ],
        out_shape=[jax.ShapeDtypeStruct((t, n), BF16), jax.ShapeDtypeStruct((t, LANES), F32)],
        scratch_shapes=[pltpu.VMEM((tm, d), BF16)],
        compiler_params=_cparams(2),
        name="proj",
    )(x2, nw, w_main, w_small)


def _gdn_kernel(q_ref, k_ref, v_ref, sm_ref, cw_ref, alog_ref, dtb_ref, o_ref, xs_ref, st_ref, *, tc):
    w = HEADS * HEAD_DIM
    nc = tc // CHUNK

    @pl.when(pl.program_id(1) == 0)
    def _():
        xs_ref[0:8, :] = jnp.zeros((8, 3 * w), F32)
        st_ref[...] = jnp.zeros_like(st_ref)

    xs_ref[8:8 + tc, 0:w] = q_ref[...].astype(F32)
    xs_ref[8:8 + tc, w:2 * w] = k_ref[...].astype(F32)
    xs_ref[8:8 + tc, 2 * w:3 * w] = v_ref[...].astype(F32)

    ri, ci, causal, strict = _chunk_masks(tc)
    tri = jnp.where(causal, 1.0, 0.0).astype(BF16)
    eye = jnp.where(ri == ci, 1.0, 0.0)

    sm = sm_ref[...]
    lane = lax.broadcasted_iota(I32, (tc, LANES), 1)
    g_all = jnp.where(lane < HEADS, -jnp.exp(alog_ref[...]) * _softplus(sm + dtb_ref[...]), 0.0)
    beta_all = jax.nn.sigmoid(sm)
    cum = _dot_exact_lhs(tri, g_all)
    ecum = jnp.exp(cum)
    cum_t = cum.T

    def conv_part(col):
        acc = cw_ref[CONV_K - 1:CONV_K, col:col + HEAD_DIM] * xs_ref[8:8 + tc, col:col + HEAD_DIM]
        for j in range(CONV_K - 1):
            off = 8 - (CONV_K - 1) + j
            acc = acc + cw_ref[j:j + 1, col:col + HEAD_DIM] * xs_ref[off:off + tc, col:col + HEAD_DIM]
        return _silu(acc)

    def l2n(x):
        return x * lax.rsqrt(jnp.sum(x * x, axis=-1, keepdims=True) + EPS)

    for h in range(HEADS):
        q = l2n(conv_part(h * HEAD_DIM)) * HEAD_DIM ** -0.5
        k = l2n(conv_part(w + h * HEAD_DIM))
        v = conv_part(2 * w + h * HEAD_DIM)
        cc = cum[:, h:h + 1]
        cr = cum_t[h:h + 1, :]
        bc = beta_all[:, HEADS + h:HEADS + h + 1]
        ec = ecum[:, h:h + 1]
        dec = jnp.where(causal, jnp.exp(jnp.where(causal, cc - cr, 0.0)), 0.0)
        kb = k * bc
        a = jnp.where(strict, _dot_nt(kb, k) * dec, 0.0)
        x = eye - a
        p = _dot(a, a)
        n_sq = CHUNK.bit_length() - 2
        for it in range(n_sq):
            x = x + _dot(x, p)
            if it + 1 < n_sq:
                p = _dot(p, p)
        uw = _dot(x, jnp.concatenate([v * bc, kb * ec], axis=1))
        u, wm = uw[:, :HEAD_DIM], uw[:, HEAD_DIM:]
        qk = _dot_nt(q, k) * dec
        qd = q * ec
        s = st_ref[h]
        for c in range(nc):
            r0 = c * CHUNK
            rows = slice(r0, r0 + CHUNK)
            last = cc[r0 + CHUNK - 1:r0 + CHUNK, :]
            kd = k[rows] * jnp.exp(last - cc[rows])
            v_new = u[rows] - _dot(wm[rows], s)
            o = _dot(qd[rows], s) + _dot(qk[rows, rows], v_new)
            s = s * jnp.exp(last) + _dot_tn(kd, v_new)
            o_ref[rows, h * HEAD_DIM:(h + 1) * HEAD_DIM] = _rms(o).astype(BF16)
        st_ref[h] = s

    xs_ref[0:8, :] = xs_ref[tc:tc + 8, :]


def _gdn(p3, small3, conv_w, alog, dtb, tc):
    b, s, _ = p3.shape
    w = HEADS * HEAD_DIM
    return pl.pallas_call(
        functools.partial(_gdn_kernel, tc=tc),
        grid=(b, s // tc),
        in_specs=[
            pl.BlockSpec((None, tc, w), lambda i, j: (i, j, 0)),
            pl.BlockSpec((None, tc, w), lambda i, j: (i, j, 1)),
            pl.BlockSpec((None, tc, w), lambda i, j: (i, j, 2)),
            pl.BlockSpec((None, tc, LANES), lambda i, j: (i, j, 0)),
            pl.BlockSpec((CONV_K, 3 * w), lambda i, j: (0, 0)),
            pl.BlockSpec((1, LANES), lambda i, j: (0, 0)),
            pl.BlockSpec((1, LANES), lambda i, j: (0, 0)),
        ],
        out_specs=pl.BlockSpec((None, tc, w), lambda i, j: (i, j, 0)),
        out_shape=jax.ShapeDtypeStruct((b, s, w), BF16),
        scratch_shapes=[pltpu.VMEM((tc + 8, 3 * w), F32), pltpu.VMEM((HEADS, HEAD_DIM, HEAD_DIM), F32)],
        compiler_params=_cparams(2),
        name="gdn",
    )(p3, p3, p3, small3, conv_w, alog, dtb)


def _hgrn_kernel(f_ref, q_ref, i_ref, lb_ref, o_ref, st_ref, *, tc):
    nc = tc // CHUNK

    @pl.when(pl.program_id(1) == 0)
    def _():
        st_ref[...] = jnp.zeros_like(st_ref)

    _, _, causal, _ = _chunk_masks(tc)
    tri = jnp.where(causal, 1.0, 0.0).astype(BF16)

    lbp = lb_ref[...]
    lbe = jnp.exp(lbp - jnp.max(lbp, axis=0, keepdims=True))
    lb = lbe[0:1, :] / jnp.sum(lbe, axis=0, keepdims=True)

    for h in range(HEADS):
        cols = slice(h * HEAD_DIM, (h + 1) * HEAD_DIM)
        lbh = lb[:, cols]
        forget = lbh + (1.0 - lbh) * jax.nn.sigmoid(f_ref[:, cols].astype(F32))
        cum = _dot_exact_lhs(tri, jnp.log(forget))
        kk = 1.0 - forget
        q_in = q_ref[:, cols].astype(F32) * HEAD_DIM ** -0.5 * jnp.exp(cum)
        k_in = kk * jnp.exp(-cum)
        v = i_ref[:, cols].astype(F32)
        intra = jnp.where(causal, _dot_nt(q_in, k_in), 0.0)
        st = st_ref[h]
        for c in range(nc):
            r0 = c * CHUNK
            rows = slice(r0, r0 + CHUNK)
            last = cum[r0 + CHUNK - 1:r0 + CHUNK, :]
            k_dec = kk[rows] * jnp.exp(last - cum[rows])
            o = _dot_nt(q_in[rows], st) + _dot(intra[rows, rows], v[rows])
            st = st * jnp.exp(last) + _dot_tn(v[rows], k_dec)
            o_ref[rows, cols] = _rms(o).astype(BF16)
        st_ref[h] = st


def _hgrn(p3, lb_logits, tc):
    b, s, _ = p3.shape
    w = HEADS * HEAD_DIM
    return pl.pallas_call(
        functools.partial(_hgrn_kernel, tc=tc),
        grid=(b, s // tc),
        in_specs=[
            pl.BlockSpec((None, tc, w), lambda i, j: (i, j, 4)),
            pl.BlockSpec((None, tc, w), lambda i, j: (i, j, 5)),
            pl.BlockSpec((None, tc, w), lambda i, j: (i, j, 6)),
            pl.BlockSpec(lb_logits.shape, lambda i, j: (0, 0)),
        ],
        out_specs=pl.BlockSpec((None, tc, w), lambda i, j: (i, j, 0)),
        out_shape=jax.ShapeDtypeStruct((b, s, w), BF16),
        scratch_shapes=[pltpu.VMEM((HEADS, HEAD_DIM, HEAD_DIM), F32)],
        compiler_params=_cparams(2),
        name="hgrn",
    )(p3, p3, p3, lb_logits)


def _kv_kernel(mem_ref, nw_ref, wkv_ref, kv_ref):
    mn = (_rms(mem_ref[...]) * nw_ref[...]).astype(BF16)
    kv_ref[...] = jnp.dot(mn, wkv_ref[...], preferred_element_type=F32).astype(BF16)


def _kv(mem, nw, wkv):
    b, m, d = mem.shape
    return pl.pallas_call(
        _kv_kernel,
        grid=(b,),
        in_specs=[
            pl.BlockSpec((None, m, d), lambda i: (i, 0, 0)),
            pl.BlockSpec((1, d), lambda i: (0, 0)),
            pl.BlockSpec(wkv.shape, lambda i: (0, 0)),
        ],
        out_specs=pl.BlockSpec((None, m, 2 * d), lambda i: (i, 0, 0)),
        out_shape=jax.ShapeDtypeStruct((b, m, 2 * d), BF16),
        compiler_params=_cparams(1),
        name="kv",
    )(mem, nw, wkv)


def _post_kernel(x_ref, oa_ref, ob_ref, oga_ref, ogb_ref, ga_ref, gb_ref, kv_ref,
                 gnw_ref, hnw_ref, wa_ref, wb_ref, wout_ref, nx_ref, wq_ref, wo_ref, nf_ref,
                 wr_ref, br_ref, h2_ref, hn3_ref, lg_ref):
    d = x_ref.shape[-1]
    dh = d // XA_HEADS
    ya = oa_ref[...].astype(F32) * gnw_ref[...] * _silu(oga_ref[...].astype(F32))
    yb = ob_ref[...].astype(F32) * hnw_ref[...] * _silu(ogb_ref[...].astype(F32))
    merged = (jax.nn.sigmoid(ga_ref[...].astype(F32)) * _dot(ya, wa_ref[...])
              + jax.nn.sigmoid(gb_ref[...].astype(F32)) * _dot(yb, wb_ref[...]))
    h1 = x_ref[...] + _dot(merged, wout_ref[...])

    q = _dot(_rms(h1) * nx_ref[...], wq_ref[...]) * dh ** -0.5
    outs = []
    for hh in range(XA_HEADS):
        kh = kv_ref[:, hh * dh:(hh + 1) * dh]
        vh = kv_ref[:, d + hh * dh:d + (hh + 1) * dh]
        sc = _dot_nt(q[:, hh * dh:(hh + 1) * dh], kh)
        p = jnp.exp(sc - jnp.max(sc, axis=-1, keepdims=True))
        outs.append(_dot(p, vh) / jnp.sum(p, axis=-1, keepdims=True))
    h2 = h1 + _dot(jnp.concatenate(outs, axis=1), wo_ref[...])
    h2_ref[...] = h2

    hn3 = _rms(h2) * nf_ref[...]
    hn3_ref[...] = hn3
    hi, lo = _split(hn3)
    whi, wlo = _split(wr_ref[...])
    lg_ref[...] = (jnp.dot(hi, whi, preferred_element_type=F32)
                   + jnp.dot(hi, wlo, preferred_element_type=F32)
                   + jnp.dot(lo, whi, preferred_element_type=F32)) + br_ref[...]


def _post(x3, oa, ob, p3, kv, gnw, hnw, wa, wb, wout, nx, wq, wo, nf, wr, br, tm):
    b, s, d = x3.shape
    row = lambda c: pl.BlockSpec((None, tm, d), lambda i, j: (i, j, c))
    full = lambda a: pl.BlockSpec(a.shape, lambda i, j: (0,) * a.ndim)
    return pl.pallas_call(
        _post_kernel,
        grid=(b, s // tm),
        in_specs=[row(0), row(0), row(0), row(3), row(7), row(8), row(9),
                  pl.BlockSpec((None,) + kv.shape[1:], lambda i, j: (i, 0, 0)),
                  full(gnw), full(hnw), full(wa), full(wb), full(wout), full(nx), full(wq), full(wo),
                  full(nf), full(wr), full(br)],
        out_specs=[row(0), row(0), pl.BlockSpec((None, tm, LANES), lambda i, j: (i, j, 0))],
        out_shape=[jax.ShapeDtypeStruct((b, s, d), F32), jax.ShapeDtypeStruct((b, s, d), F32),
                   jax.ShapeDtypeStruct((b, s, LANES), F32)],
        compiler_params=_cparams(2),
        name="post",
    )(x3, oa, ob, p3, p3, p3, p3, kv, gnw, hnw, wa, wb, wout, nx, wq, wo, nf, wr, br)


def _route_kernel(lg_ref, code_ref, wt_ref, cnt_ref, carry_ref, *, tr):
    @pl.when(pl.program_id(0) == 0)
    def _():
        carry_ref[...] = jnp.zeros_like(carry_ref)

    neg = -1e30
    big = 2 * LANES
    lg = lg_ref[...]
    lane = lax.broadcasted_iota(I32, (tr, LANES), 1)
    lane_f = lane.astype(F32)
    first = lambda m: jnp.min(jnp.where(m, lane_f, big), axis=-1, keepdims=True).astype(I32)

    is_g = lane < N_GROUPS
    gl = jnp.where(is_g, lg, neg)
    gmax = jnp.max(gl, axis=-1, keepdims=True)
    gidx = first(gl == gmax)
    g_p = 1.0 / jnp.sum(jnp.where(is_g, jnp.exp(gl - gmax), 0.0), axis=-1, keepdims=True)

    lo_lane = N_GROUPS + gidx * EXP_PER_GROUP
    in_grp = (lane >= lo_lane) & (lane < lo_lane + EXP_PER_GROUP)
    el = jnp.where(in_grp, lg, neg)
    m1 = jnp.max(el, axis=-1, keepdims=True)
    i1 = first(el == m1)
    el2 = jnp.where(lane == i1, neg, el)
    m2 = jnp.max(el2, axis=-1, keepdims=True)
    i2 = first(el2 == m2)
    esum = jnp.sum(jnp.where(in_grp, jnp.exp(el - m1), 0.0), axis=-1, keepdims=True)
    p1 = 1.0 / esum
    p2 = jnp.exp(m2 - m1) / esum
    w1 = g_p * p1 / (p1 + p2)
    w2 = g_p * p2 / (p1 + p2)
    e1 = i1 - N_GROUPS
    e2 = i2 - N_GROUPS

    oh1 = lane == e1
    oh2 = lane == e2
    both = jnp.where(oh1 | oh2, 1.0, 0.0)
    ri = lax.broadcasted_iota(I32, (tr, tr), 0)
    ci = lax.broadcasted_iota(I32, (tr, tr), 1)
    below = jnp.where(ri > ci, 1.0, 0.0).astype(BF16)
    before = jnp.dot(below, both.astype(BF16), preferred_element_type=F32) + carry_ref[...]
    r1 = jnp.sum(jnp.where(oh1, before, 0.0), axis=-1, keepdims=True).astype(I32)
    r2 = jnp.sum(jnp.where(oh2, before, 0.0), axis=-1, keepdims=True).astype(I32)
    carry_ref[...] = carry_ref[...] + jnp.sum(both, axis=0, keepdims=True)

    code_ref[...] = jnp.where(lane == 0, e1 * (1 << RANK_BITS) + r1,
                              jnp.where(lane == 1, e2 * (1 << RANK_BITS) + r2, 0))
    wt_ref[...] = jnp.where(lane == 0, w1, jnp.where(lane == 1, w2, 0.0))
    cnt_ref[...] = carry_ref[...]


def _route(logits, tr):
    t = logits.shape[0]
    return pl.pallas_call(
        functools.partial(_route_kernel, tr=tr),
        grid=(t // tr,),
        in_specs=[pl.BlockSpec((tr, LANES), lambda i: (i, 0))],
        out_specs=[pl.BlockSpec((tr, LANES), lambda i: (i, 0)),
                   pl.BlockSpec((tr, LANES), lambda i: (i, 0)),
                   pl.BlockSpec((1, LANES), lambda i: (0, 0))],
        out_shape=[jax.ShapeDtypeStruct((t, LANES), I32), jax.ShapeDtypeStruct((t, LANES), F32),
                   jax.ShapeDtypeStruct((1, LANES), F32)],
        scratch_shapes=[pltpu.VMEM((1, LANES), F32)],
        compiler_params=_cparams(1),
        name="route",
    )(logits)


def _dest(ps_ref, code):
    return ps_ref[code >> RANK_BITS] + (code & ((1 << RANK_BITS) - 1))


def _dispatch_kernel(ps_ref, c1_ref, c2_ref, hn_ref, xz_ref, xp_ref, sem, *, tr):
    del xz_ref
    base = pl.program_id(0) * tr

    def copy(i, d):
        return pltpu.make_async_copy(hn_ref.at[pl.ds(i, 1)], xp_ref.at[pl.ds(d, 1)], sem)

    def issue(i, carry):
        copy(i, _dest(ps_ref, c1_ref[base + i])).start()
        copy(i, _dest(ps_ref, c2_ref[base + i])).start()
        return carry

    lax.fori_loop(0, tr, issue, 0)

    def drain(i, carry):
        copy(0, 0).wait()
        copy(0, 0).wait()
        return carry

    lax.fori_loop(0, tr, drain, 0)


def _dispatch(pstart, code1, code2, hn3, n_rows, tr):
    t, d = hn3.shape
    zeros = jnp.zeros((n_rows, d), F32)
    return pl.pallas_call(
        functools.partial(_dispatch_kernel, tr=tr),
        grid_spec=pltpu.PrefetchScalarGridSpec(
            num_scalar_prefetch=3,
            grid=(t // tr,),
            in_specs=[pl.BlockSpec((tr, d), lambda i, *_: (i, 0)),
                      pl.BlockSpec(memory_space=pl.ANY)],
            out_specs=pl.BlockSpec(memory_space=pl.ANY),
            scratch_shapes=[pltpu.SemaphoreType.DMA],
        ),
        out_shape=jax.ShapeDtypeStruct((n_rows, d), F32),
        input_output_aliases={4: 0},
        compiler_params=_cparams(1),
        name="dispatch",
    )(pstart, code1, code2, hn3, zeros)


def _expert_kernel(be_ref, nu_ref, x_ref, wg_ref, wu_ref, wd_ref, y_ref):
    del be_ref

    @pl.when(pl.program_id(0) < nu_ref[0])
    def _():
        x = x_ref[...]
        hmid = _silu(_dot(x, wg_ref[...])) * _dot(x, wu_ref[...])
        y_ref[...] = _dot(hmid, wd_ref[...])

    @pl.when(pl.program_id(0) >= nu_ref[0])
    def _():
        y_ref[...] = jnp.zeros_like(y_ref)


def _experts(blk_e, n_used, x_pad, wg, wu, wd):
    n_rows, d = x_pad.shape
    ff = wg.shape[-1]
    n_blocks = n_rows // MOE_BLOCK
    rows = lambda i, be, nu: (jnp.minimum(i, nu[0] - 1), 0)
    return pl.pallas_call(
        _expert_kernel,
        grid_spec=pltpu.PrefetchScalarGridSpec(
            num_scalar_prefetch=2,
            grid=(n_blocks,),
            in_specs=[pl.BlockSpec((MOE_BLOCK, d), rows),
                      pl.BlockSpec((None, d, ff), lambda i, be, nu: (be[i], 0, 0)),
                      pl.BlockSpec((None, d, ff), lambda i, be, nu: (be[i], 0, 0)),
                      pl.BlockSpec((None, ff, d), lambda i, be, nu: (be[i], 0, 0))],
            out_specs=pl.BlockSpec((MOE_BLOCK, d), lambda i, be, nu: (i, 0)),
        ),
        out_shape=jax.ShapeDtypeStruct((n_rows, d), F32),
        compiler_params=_cparams(1),
        name="experts",
    )(blk_e, n_used, x_pad, wg, wu, wd)


def _combine_kernel(ps_ref, c1_ref, c2_ref, h2_ref, wt_ref, fnw_ref, y_ref, out_ref, b1_ref, b2_ref, sem, *, tr):
    base = pl.program_id(0) * tr

    def copy(d, buf, i):
        return pltpu.make_async_copy(y_ref.at[pl.ds(d, 1)], buf.at[pl.ds(i, 1)], sem)

    def issue(i, carry):
        copy(_dest(ps_ref, c1_ref[base + i]), b1_ref, i).start()
        copy(_dest(ps_ref, c2_ref[base + i]), b2_ref, i).start()
        return carry

    lax.fori_loop(0, tr, issue, 0)

    def drain(i, carry):
        copy(0, b1_ref, 0).wait()
        copy(0, b2_ref, 0).wait()
        return carry

    lax.fori_loop(0, tr, drain, 0)

    wt = wt_ref[...]
    h3 = h2_ref[...] + wt[:, 0:1] * b1_ref[...] + wt[:, 1:2] * b2_ref[...]
    out_ref[...] = _rms(h3) * fnw_ref[...]


def _combine(pstart, code1, code2, h2, wts, fnw, y_pad, tr):
    t, d = h2.shape
    return pl.pallas_call(
        functools.partial(_combine_kernel, tr=tr),
        grid_spec=pltpu.PrefetchScalarGridSpec(
            num_scalar_prefetch=3,
            grid=(t // tr,),
            in_specs=[pl.BlockSpec((tr, d), lambda i, *_: (i, 0)),
                      pl.BlockSpec((tr, LANES), lambda i, *_: (i, 0)),
                      pl.BlockSpec((1, d), lambda i, *_: (0, 0)),
                      pl.BlockSpec(memory_space=pl.ANY)],
            out_specs=pl.BlockSpec((tr, d), lambda i, *_: (i, 0)),
            scratch_shapes=[pltpu.VMEM((tr, d), F32), pltpu.VMEM((tr, d), F32), pltpu.SemaphoreType.DMA],
        ),
        out_shape=jax.ShapeDtypeStruct((t, d), F32),
        compiler_params=_cparams(1),
        name="combine",
    )(pstart, code1, code2, h2, wts, fnw, y_pad)


def _pick(n, pref):
    while n % pref:
        pref //= 2
    return pref


def kernel(x, mem, norm_mix_w, w_in, conv_w, gdn_a_log, gdn_dt_bias, gdn_out_norm_w, hgrn_lb, hgrn_out_norm_w, w_branch_a, w_branch_b, w_out, norm_xattn_w, norm_mem_w, xattn_wq, xattn_wkv, xattn_wo, norm_ffn_w, router_group_w, router_group_b, router_expert_w, router_expert_b, expert_w_gate, expert_w_up, expert_w_down, final_norm_w):
    b, s, d = x.shape
    t = b * s
    depth = w_in.shape[0]
    w = HEADS * HEAD_DIM
    qkv_w = 3 * w
    assert d == w and s % CHUNK == 0

    tc = _pick(s, 256)
    tm_proj = _pick(t, 1024)
    tm_post = _pick(s, 256)
    tr = _pick(t, 256)

    def pad_lanes(v, offset=0):
        return jnp.zeros((1, LANES), F32).at[0, offset:offset + v.shape[0]].set(v.astype(F32))

    assert depth == 1
    h3d = x
    for layer in range(depth):
        wl = w_in[layer]
        w_main = jnp.concatenate([wl[:, :qkv_w], wl[:, qkv_w + 2 * HEADS:]], axis=1).astype(BF16)
        w_small = jnp.zeros((d, LANES), F32).at[:, :2 * HEADS].set(wl[:, qkv_w:qkv_w + 2 * HEADS]).astype(BF16)
        p_main, p_small = _proj(h3d.reshape(t, d), norm_mix_w[layer][None, :], w_main, w_small,
                                tm_proj, 1024)
        p3 = p_main.reshape(b, s, -1)
        o_a = _gdn(p3, p_small.reshape(b, s, LANES), conv_w[layer], pad_lanes(gdn_a_log[layer]),
                   pad_lanes(gdn_dt_bias[layer]), tc)
        o_b = _hgrn(p3, hgrn_lb[layer:], tc)
        kv = _kv(mem, norm_mem_w[layer][None, :], xattn_wkv[layer].astype(BF16))
        w_router = jnp.zeros((d, LANES), F32)
        w_router = w_router.at[:, :N_GROUPS].set(router_group_w[layer])
        w_router = w_router.at[:, N_GROUPS:N_GROUPS + N_EXPERTS].set(router_expert_w[layer])
        b_router = pad_lanes(router_group_b[layer]) + pad_lanes(router_expert_b[layer], N_GROUPS)
        tile8 = lambda v: jnp.tile(v.astype(F32), HEADS)[None, :]
        h2, hn3, logits = _post(
            h3d, o_a, o_b, p3, kv, tile8(gdn_out_norm_w[layer]), tile8(hgrn_out_norm_w[layer]),
            w_branch_a[layer].astype(BF16), w_branch_b[layer].astype(BF16), w_out[layer].astype(BF16),
            norm_xattn_w[layer][None, :], xattn_wq[layer].astype(BF16), xattn_wo[layer].astype(BF16),
            norm_ffn_w[layer][None, :], w_router, b_router, tm_post)

        codes, wts, counts = _route(logits.reshape(t, LANES), tr)
        sizes = counts[0, :N_EXPERTS].astype(I32)
        padded = ((sizes + MOE_BLOCK - 1) // MOE_BLOCK) * MOE_BLOCK
        pend = jnp.cumsum(padded)
        pstart = (pend - padded).astype(I32)
        m = t * 2
        n_rows = ((m + MOE_BLOCK - 1) // MOE_BLOCK) * MOE_BLOCK + N_EXPERTS * MOE_BLOCK
        n_blocks = n_rows // MOE_BLOCK
        blk_start = jnp.arange(n_blocks, dtype=I32) * MOE_BLOCK
        blk_e = jnp.clip(jnp.searchsorted(pend, blk_start, side='right'), 0, N_EXPERTS - 1).astype(I32)
        n_used = (pend[-1:] // MOE_BLOCK).astype(I32)
        code1, code2 = codes[:, 0], codes[:, 1]

        x_pad = _dispatch(pstart, code1, code2, hn3.reshape(t, d), n_rows, tr)
        y_pad = _experts(blk_e, n_used, x_pad, expert_w_gate[layer], expert_w_up[layer], expert_w_down[layer])
        out =_combine(pstart, code1, code2, h2.reshape(t, d), wts, final_norm_w[None, :], y_pad, tr)
        h3d = out.reshape(b, s, d)
    return h3d
```

```python
import functools

import jax
import jax.numpy as jnp
from jax import lax
from jax.experimental import pallas as pl
from jax.experimental.pallas import tpu as pltpu

F32 = jnp.float32
BF16 = jnp.bfloat16
I32 = jnp.int32

EPS = 1e-6
CHUNK = 64
HEADS = 8
HEAD_DIM = 128
CONV_K = 4
XA_HEADS = 4
N_GROUPS = 4
EXP_PER_GROUP = 8
N_EXPERTS = N_GROUPS * EXP_PER_GROUP
MOE_BLOCK = 256
LANES = 128
RANK_BITS = 20

VMEM_LIMIT = 52 * 1024 * 1024


def _cparams(n_axes):
    return pltpu.CompilerParams(dimension_semantics=("arbitrary",) * n_axes,
                                vmem_limit_bytes=VMEM_LIMIT)


def _dot(a, b):
    return jnp.dot(a.astype(BF16), b.astype(BF16), preferred_element_type=F32)


def _dot_nt(a, b):
    return lax.dot_general(a.astype(BF16), b.astype(BF16), (((1,), (1,)), ((), ())),
                           preferred_element_type=F32)


def _dot_tn(a, b):
    return lax.dot_general(a.astype(BF16), b.astype(BF16), (((0,), (0,)), ((), ())),
                           preferred_element_type=F32)


def _split(x):
    hi = x.astype(BF16)
    lo = (x - hi.astype(F32)).astype(BF16)
    return hi, lo


def _dot_exact_lhs(m_bf16, x):
    hi, lo = _split(x)
    return (jnp.dot(m_bf16, hi, preferred_element_type=F32)
            + jnp.dot(m_bf16, lo, preferred_element_type=F32))


def _rms(x):
    return x * lax.rsqrt(jnp.mean(x * x, axis=-1, keepdims=True) + EPS)


def _silu(x):
    return x * jax.nn.sigmoid(x)


def _softplus(x):
    return jnp.maximum(x, 0.0) + jnp.log(1.0 + jnp.exp(-jnp.abs(x)))


def _chunk_masks(tc):
    ri = lax.broadcasted_iota(I32, (tc, tc), 0)
    ci = lax.broadcasted_iota(I32, (tc, tc), 1)
    same = (ri // CHUNK) == (ci // CHUNK)
    causal = same & (ri >= ci)
    strict = same & (ri > ci)
    return ri, ci, causal, strict


def _proj_kernel(x_ref, nw_ref, w_ref, ws_ref, out_ref, small_ref, hn_ref):
    @pl.when(pl.program_id(1) == 0)
    def _():
        hn = (_rms(x_ref[...]) * nw_ref[...]).astype(BF16)
        hn_ref[...] = hn
        small_ref[...] = jnp.dot(hn, ws_ref[...], preferred_element_type=F32)

    out_ref[...] = jnp.dot(hn_ref[...], w_ref[...], preferred_element_type=F32).astype(BF16)


def _proj(x2, nw, w_main, w_small, tm, tn):
    t, d = x2.shape
    n = w_main.shape[1]
    return pl.pallas_call(
        _proj_kernel,
        grid=(t // tm, n // tn),
        in_specs=[
            pl.BlockSpec((tm, d), lambda i, j: (i, 0)),
            pl.BlockSpec((1, d), lambda i, j: (0, 0)),
            pl.BlockSpec((d, tn), lambda i, j: (0, j)),
            pl.BlockSpec((d, LANES), lambda i, j: (0, 0)),
        ],
        out_specs=[
            pl.BlockSpec((tm, tn), lambda i, j: (i, j)),
            pl.BlockSpec((tm, LANES), lambda i, j: (i, 0)),
        ],
        out_shape=[jax.ShapeDtypeStruct((t, n), BF16), jax.ShapeDtypeStruct((t, LANES), F32)],
        scratch_shapes=[pltpu.VMEM((tm, d), BF16)],
        compiler_params=_cparams(2),
        name="proj",
    )(x2, nw, w_main, w_small)


def _gdn_kernel(q_ref, k_ref, v_ref, sm_ref, cw_ref, alog_ref, dtb_ref, o_ref, xs_ref, st_ref, *, tc):
    w = HEADS * HEAD_DIM
    nc = tc // CHUNK

    @pl.when(pl.program_id(1) == 0)
    def _():
        xs_ref[0:8, :] = jnp.zeros((8, 3 * w), F32)
        st_ref[...] = jnp.zeros_like(st_ref)

    xs_ref[8:8 + tc, 0:w] = q_ref[...].astype(F32)
    xs_ref[8:8 + tc, w:2 * w] = k_ref[...].astype(F32)
    xs_ref[8:8 + tc, 2 * w:3 * w] = v_ref[...].astype(F32)

    _, _, causal, strict = _chunk_masks(tc)
    tri = jnp.where(causal, 1.0, 0.0).astype(BF16)
    wi = lax.broadcasted_iota(I32, (CHUNK, tc), 0)
    wj = lax.broadcasted_iota(I32, (CHUNK, tc), 1)
    eye_w = jnp.where(wi == wj % CHUNK, 1.0, 0.0)
    blk_w = wj // CHUNK

    def fold(m_bd):
        acc = m_bd[0:CHUNK]
        for c in range(1, nc):
            acc = acc + m_bd[c * CHUNK:(c + 1) * CHUNK]
        return acc

    def unfold(m_w):
        return jnp.concatenate([jnp.where(blk_w == c, m_w, 0.0) for c in range(nc)], axis=0)

    sm = sm_ref[...]
    lane = lax.broadcasted_iota(I32, (tc, LANES), 1)
    g_all = jnp.where(lane < HEADS, -jnp.exp(alog_ref[...]) * _softplus(sm + dtb_ref[...]), 0.0)
    beta_all = jax.nn.sigmoid(sm)
    cum = _dot_exact_lhs(tri, g_all)
    ecum = jnp.exp(cum)
    cum_t = cum.T

    def conv_part(col):
        acc = cw_ref[CONV_K - 1:CONV_K, col:col + HEAD_DIM] * xs_ref[8:8 + tc, col:col + HEAD_DIM]
        for j in range(CONV_K - 1):
            off = 8 - (CONV_K - 1) + j
            acc = acc + cw_ref[j:j + 1, col:col + HEAD_DIM] * xs_ref[off:off + tc, col:col + HEAD_DIM]
        return _silu(acc)

    def l2n(x):
        return x * lax.rsqrt(jnp.sum(x * x, axis=-1, keepdims=True) + EPS)

    heads = range(HEADS)
    ks, a_bd, qk_bd, rhs, qd, cc = [], [], [], [], [], []
    for h in heads:
        q = l2n(conv_part(h * HEAD_DIM)) * HEAD_DIM ** -0.5
        k = l2n(conv_part(w + h * HEAD_DIM))
        v = conv_part(2 * w + h * HEAD_DIM)
        cch = cum[:, h:h + 1]
        cr = cum_t[h:h + 1, :]
        bc = beta_all[:, HEADS + h:HEADS + h + 1]
        ec = ecum[:, h:h + 1]
        dec = jnp.where(causal, jnp.exp(jnp.where(causal, cch - cr, 0.0)), 0.0)
        kb = k * bc
        a_bd.append(jnp.where(strict, _dot_nt(kb, k) * dec, 0.0))
        qk_bd.append(_dot_nt(q, k) * dec)
        rhs.append(jnp.concatenate([v * bc, kb * ec], axis=1).astype(BF16))
        qd.append(q * ec)
        ks.append(k)
        cc.append(cch)

    xw, pw = [], []
    for h in heads:
        aw = fold(a_bd[h])
        xw.append(eye_w - aw)
        pw.append(_dot(aw, a_bd[h]))
    n_sq = CHUNK.bit_length() - 2
    for it in range(n_sq):
        for h in heads:
            p_bd = unfold(pw[h]).astype(BF16)
            if it + 1 < n_sq:
                res = _dot(jnp.concatenate([xw[h], pw[h]], axis=0), p_bd)
                xw[h] = xw[h] + res[:CHUNK]
                pw[h] = res[CHUNK:]
            else:
                xw[h] = xw[h] + _dot(xw[h], p_bd)

    uw, qp, o0 = [], [], []
    for h in heads:
        uwh = _dot(unfold(xw[h]), rhs[h])
        qkuw = _dot(qk_bd[h], uwh)
        uw.append(uwh.astype(BF16))
        o0.append(qkuw[:, :HEAD_DIM])
        qp.append(qd[h] - qkuw[:, HEAD_DIM:])

    state = [st_ref[h] for h in heads]
    for c in range(nc):
        r0 = c * CHUNK
        rows = slice(r0, r0 + CHUNK)
        kuw, dlast = [], []
        for h in heads:
            last = cc[h][r0 + CHUNK - 1:r0 + CHUNK, :]
            kd = ks[h][rows] * jnp.exp(last - cc[h][rows])
            kuw.append(_dot_tn(kd, uw[h][rows]))
            dlast.append(jnp.exp(last))
        for h in heads:
            s = state[h]
            res = _dot(jnp.concatenate([kuw[h][:, HEAD_DIM:], qp[h][rows]], axis=0), s)
            o = res[HEAD_DIM:] + o0[h][rows]
            state[h] = s * dlast[h] - res[:HEAD_DIM] + kuw[h][:, :HEAD_DIM]
            o_ref[rows, h * HEAD_DIM:(h + 1) * HEAD_DIM] = _rms(o).astype(BF16)
    for h in heads:
        st_ref[h] = state[h]

    xs_ref[0:8, :] = xs_ref[tc:tc + 8, :]


def _gdn(p3, small3, conv_w, alog, dtb, tc):
    b, s, _ = p3.shape
    w = HEADS * HEAD_DIM
    return pl.pallas_call(
        functools.partial(_gdn_kernel, tc=tc),
        grid=(b, s // tc),
        in_specs=[
            pl.BlockSpec((None, tc, w), lambda i, j: (i, j, 0)),
            pl.BlockSpec((None, tc, w), lambda i, j: (i, j, 1)),
            pl.BlockSpec((None, tc, w), lambda i, j: (i, j, 2)),
            pl.BlockSpec((None, tc, LANES), lambda i, j: (i, j, 0)),
            pl.BlockSpec((CONV_K, 3 * w), lambda i, j: (0, 0)),
            pl.BlockSpec((1, LANES), lambda i, j: (0, 0)),
            pl.BlockSpec((1, LANES), lambda i, j: (0, 0)),
        ],
        out_specs=pl.BlockSpec((None, tc, w), lambda i, j: (i, j, 0)),
        out_shape=jax.ShapeDtypeStruct((b, s, w), BF16),
        scratch_shapes=[pltpu.VMEM((tc + 8, 3 * w), F32), pltpu.VMEM((HEADS, HEAD_DIM, HEAD_DIM), F32)],
        compiler_params=_cparams(2),
        name="gdn",
    )(p3, p3, p3, small3, conv_w, alog, dtb)


def _hgrn_kernel(f_ref, q_ref, i_ref, lb_ref, o_ref, st_ref, *, tc):
    nc = tc // CHUNK

    @pl.when(pl.program_id(1) == 0)
    def _():
        st_ref[...] = jnp.zeros_like(st_ref)

    _, _, causal, _ = _chunk_masks(tc)
    tri = jnp.where(causal, 1.0, 0.0).astype(BF16)

    lbp = lb_ref[...]
    lbe = jnp.exp(lbp - jnp.max(lbp, axis=0, keepdims=True))
    lb = lbe[0:1, :] / jnp.sum(lbe, axis=0, keepdims=True)

    heads = range(HEADS)
    hcols = [slice(h * HEAD_DIM, (h + 1) * HEAD_DIM) for h in heads]
    forget = lb + (1.0 - lb) * jax.nn.sigmoid(f_ref[...].astype(F32))
    cum = _dot_exact_lhs(tri, jnp.log(forget))
    kk = 1.0 - forget
    q_in = (q_ref[...].astype(F32) * HEAD_DIM ** -0.5 * jnp.exp(cum)).astype(BF16)
    k_in = (kk * jnp.exp(-cum)).astype(BF16)
    v = i_ref[...]
    intra = [jnp.where(causal, _dot_nt(q_in[:, hc], k_in[:, hc]), 0.0).astype(BF16) for hc in hcols]
    state = [st_ref[h] for h in heads]
    for c in range(nc):
        r0 = c * CHUNK
        rows = slice(r0, r0 + CHUNK)
        last = cum[r0 + CHUNK - 1:r0 + CHUNK, :]
        k_dec = (kk[rows] * jnp.exp(last - cum[rows])).astype(BF16)
        dlast = jnp.exp(last)
        for h, hc in zip(heads, hcols):
            o = _dot_nt(q_in[rows, hc], state[h]) + _dot(intra[h][rows, rows], v[rows, hc])
            state[h] = state[h] * dlast[:, hc] + _dot_tn(v[rows, hc], k_dec[:, hc])
            o_ref[rows, hc] = _rms(o).astype(BF16)
    for h in heads:
        st_ref[h] = state[h]


def _hgrn(p3, lb_logits, tc):
    b, s, _ = p3.shape
    w = HEADS * HEAD_DIM
    return pl.pallas_call(
        functools.partial(_hgrn_kernel, tc=tc),
        grid=(b, s // tc),
        in_specs=[
            pl.BlockSpec((None, tc, w), lambda i, j: (i, j, 4)),
            pl.BlockSpec((None, tc, w), lambda i, j: (i, j, 5)),
            pl.BlockSpec((None, tc, w), lambda i, j: (i, j, 6)),
            pl.BlockSpec(lb_logits.shape, lambda i, j: (0, 0)),
        ],
        out_specs=pl.BlockSpec((None, tc, w), lambda i, j: (i, j, 0)),
        out_shape=jax.ShapeDtypeStruct((b, s, w), BF16),
        scratch_shapes=[pltpu.VMEM((HEADS, HEAD_DIM, HEAD_DIM), F32)],
        compiler_params=_cparams(2),
        name="hgrn",
    )(p3, p3, p3, lb_logits)


def _kv_kernel(mem_ref, nw_ref, wkv_ref, kv_ref):
    mn = (_rms(mem_ref[...]) * nw_ref[...]).astype(BF16)
    kv_ref[...] = jnp.dot(mn, wkv_ref[...], preferred_element_type=F32).astype(BF16)


def _kv(mem, nw, wkv):
    b, m, d = mem.shape
    return pl.pallas_call(
        _kv_kernel,
        grid=(b,),
        in_specs=[
            pl.BlockSpec((None, m, d), lambda i: (i, 0, 0)),
            pl.BlockSpec((1, d), lambda i: (0, 0)),
            pl.BlockSpec(wkv.shape, lambda i: (0, 0)),
        ],
        out_specs=pl.BlockSpec((None, m, 2 * d), lambda i: (i, 0, 0)),
        out_shape=jax.ShapeDtypeStruct((b, m, 2 * d), BF16),
        compiler_params=_cparams(1),
        name="kv",
    )(mem, nw, wkv)


def _post_kernel(x_ref, oa_ref, ob_ref, oga_ref, ogb_ref, ga_ref, gb_ref, kv_ref,
                 gnw_ref, hnw_ref, wa_ref, wb_ref, wout_ref, nx_ref, wq_ref, wo_ref, nf_ref,
                 wr_ref, br_ref, h2_ref, hn3_ref, lg_ref):
    d = x_ref.shape[-1]
    dh = d // XA_HEADS
    ya = oa_ref[...].astype(F32) * gnw_ref[...] * _silu(oga_ref[...].astype(F32))
    yb = ob_ref[...].astype(F32) * hnw_ref[...] * _silu(ogb_ref[...].astype(F32))
    merged = (jax.nn.sigmoid(ga_ref[...].astype(F32)) * _dot(ya, wa_ref[...])
              + jax.nn.sigmoid(gb_ref[...].astype(F32)) * _dot(yb, wb_ref[...]))
    h1 = x_ref[...] + _dot(merged, wout_ref[...])

    q = _dot(_rms(h1) * nx_ref[...], wq_ref[...]) * dh ** -0.5
    outs = []
    for hh in range(XA_HEADS):
        kh = kv_ref[:, hh * dh:(hh + 1) * dh]
        vh = kv_ref[:, d + hh * dh:d + (hh + 1) * dh]
        sc = _dot_nt(q[:, hh * dh:(hh + 1) * dh], kh)
        p = jnp.exp(sc - jnp.max(sc, axis=-1, keepdims=True))
        outs.append(_dot(p, vh) / jnp.sum(p, axis=-1, keepdims=True))
    h2 = h1 + _dot(jnp.concatenate(outs, axis=1), wo_ref[...])
    h2_ref[...] = h2

    hn3 = _rms(h2) * nf_ref[...]
    hn3_ref[...] = hn3
    hi, lo = _split(hn3)
    whi, wlo = _split(wr_ref[...])
    lg_ref[...] = (jnp.dot(hi, whi, preferred_element_type=F32)
                   + jnp.dot(hi, wlo, preferred_element_type=F32)
                   + jnp.dot(lo, whi, preferred_element_type=F32)) + br_ref[...]


def _post(x3, oa, ob, p3, kv, gnw, hnw, wa, wb, wout, nx, wq, wo, nf, wr, br, tm):
    b, s, d = x3.shape
    row = lambda c: pl.BlockSpec((None, tm, d), lambda i, j: (i, j, c))
    full = lambda a: pl.BlockSpec(a.shape, lambda i, j: (0,) * a.ndim)
    return pl.pallas_call(
        _post_kernel,
        grid=(b, s // tm),
        in_specs=[row(0), row(0), row(0), row(3), row(7), row(8), row(9),
                  pl.BlockSpec((None,) + kv.shape[1:], lambda i, j: (i, 0, 0)),
                  full(gnw), full(hnw), full(wa), full(wb), full(wout), full(nx), full(wq), full(wo),
                  full(nf), full(wr), full(br)],
        out_specs=[row(0), row(0), pl.BlockSpec((None, tm, LANES), lambda i, j: (i, j, 0))],
        out_shape=[jax.ShapeDtypeStruct((b, s, d), F32), jax.ShapeDtypeStruct((b, s, d), F32),
                   jax.ShapeDtypeStruct((b, s, LANES), F32)],
        compiler_params=_cparams(2),
        name="post",
    )(x3, oa, ob, p3, p3, p3, p3, kv, gnw, hnw, wa, wb, wout, nx, wq, wo, nf, wr, br)


def _route_kernel(lg_ref, code_ref, wt_ref, cnt_ref, carry_ref, *, tr):
    @pl.when(pl.program_id(0) == 0)
    def _():
        carry_ref[...] = jnp.zeros_like(carry_ref)

    neg = -1e30
    big = 2 * LANES
    lg = lg_ref[...]
    lane = lax.broadcasted_iota(I32, (tr, LANES), 1)
    lane_f = lane.astype(F32)
    first = lambda m: jnp.min(jnp.where(m, lane_f, big), axis=-1, keepdims=True).astype(I32)

    is_g = lane < N_GROUPS
    gl = jnp.where(is_g, lg, neg)
    gmax = jnp.max(gl, axis=-1, keepdims=True)
    gidx = first(gl == gmax)
    g_p = 1.0 / jnp.sum(jnp.where(is_g, jnp.exp(gl - gmax), 0.0), axis=-1, keepdims=True)

    lo_lane = N_GROUPS + gidx * EXP_PER_GROUP
    in_grp = (lane >= lo_lane) & (lane < lo_lane + EXP_PER_GROUP)
    el = jnp.where(in_grp, lg, neg)
    m1 = jnp.max(el, axis=-1, keepdims=True)
    i1 = first(el == m1)
    el2 = jnp.where(lane == i1, neg, el)
    m2 = jnp.max(el2, axis=-1, keepdims=True)
    i2 = first(el2 == m2)
    esum = jnp.sum(jnp.where(in_grp, jnp.exp(el - m1), 0.0), axis=-1, keepdims=True)
    p1 = 1.0 / esum
    p2 = jnp.exp(m2 - m1) / esum
    w1 = g_p * p1 / (p1 + p2)
    w2 = g_p * p2 / (p1 + p2)
    e1 = i1 - N_GROUPS
    e2 = i2 - N_GROUPS

    oh1 = lane == e1
    oh2 = lane == e2
    both = jnp.where(oh1 | oh2, 1.0, 0.0)
    ri = lax.broadcasted_iota(I32, (tr, tr), 0)
    ci = lax.broadcasted_iota(I32, (tr, tr), 1)
    below = jnp.where(ri > ci, 1.0, 0.0).astype(BF16)
    before = jnp.dot(below, both.astype(BF16), preferred_element_type=F32) + carry_ref[...]
    r1 = jnp.sum(jnp.where(oh1, before, 0.0), axis=-1, keepdims=True).astype(I32)
    r2 = jnp.sum(jnp.where(oh2, before, 0.0), axis=-1, keepdims=True).astype(I32)
    carry_ref[...] = carry_ref[...] + jnp.sum(both, axis=0, keepdims=True)

    code_ref[...] = jnp.where(lane == 0, e1 * (1 << RANK_BITS) + r1,
                              jnp.where(lane == 1, e2 * (1 << RANK_BITS) + r2, 0))
    wt_ref[...] = jnp.where(lane == 0, w1, jnp.where(lane == 1, w2, 0.0))
    cnt_ref[...] = carry_ref[...]


def _route(logits, tr):
    t = logits.shape[0]
    return pl.pallas_call(
        functools.partial(_route_kernel, tr=tr),
        grid=(t // tr,),
        in_specs=[pl.BlockSpec((tr, LANES), lambda i: (i, 0))],
        out_specs=[pl.BlockSpec((tr, LANES), lambda i: (i, 0)),
                   pl.BlockSpec((tr, LANES), lambda i: (i, 0)),
                   pl.BlockSpec((1, LANES), lambda i: (0, 0))],
        out_shape=[jax.ShapeDtypeStruct((t, LANES), I32), jax.ShapeDtypeStruct((t, LANES), F32),
                   jax.ShapeDtypeStruct((1, LANES), F32)],
        scratch_shapes=[pltpu.VMEM((1, LANES), F32)],
        compiler_params=_cparams(1),
        name="route",
    )(logits)


def _dest(ps_ref, code):
    return ps_ref[code >> RANK_BITS] + (code & ((1 << RANK_BITS) - 1))


def _dispatch_kernel(ps_ref, c1_ref, c2_ref, hn_ref, xz_ref, xp_ref, sem, *, tr):
    del xz_ref
    base = pl.program_id(0) * tr

    def copy(i, d):
        return pltpu.make_async_copy(hn_ref.at[pl.ds(i, 1)], xp_ref.at[pl.ds(d, 1)], sem)

    def issue(i, carry):
        copy(i, _dest(ps_ref, c1_ref[base + i])).start()
        copy(i, _dest(ps_ref, c2_ref[base + i])).start()
        return carry

    lax.fori_loop(0, tr, issue, 0)

    def drain(i, carry):
        copy(0, 0).wait()
        copy(0, 0).wait()
        return carry

    lax.fori_loop(0, tr, drain, 0)


def _dispatch(pstart, code1, code2, hn3, n_rows, tr):
    t, d = hn3.shape
    zeros = jnp.zeros((n_rows, d), F32)
    return pl.pallas_call(
        functools.partial(_dispatch_kernel, tr=tr),
        grid_spec=pltpu.PrefetchScalarGridSpec(
            num_scalar_prefetch=3,
            grid=(t // tr,),
            in_specs=[pl.BlockSpec((tr, d), lambda i, *_: (i, 0)),
                      pl.BlockSpec(memory_space=pl.ANY)],
            out_specs=pl.BlockSpec(memory_space=pl.ANY),
            scratch_shapes=[pltpu.SemaphoreType.DMA],
        ),
        out_shape=jax.ShapeDtypeStruct((n_rows, d), F32),
        input_output_aliases={4: 0},
        compiler_params=_cparams(1),
        name="dispatch",
    )(pstart, code1, code2, hn3, zeros)


def _expert_kernel(be_ref, nu_ref, x_ref, wg_ref, wu_ref, wd_ref, y_ref):
    del be_ref

    @pl.when(pl.program_id(0) < nu_ref[0])
    def _():
        x = x_ref[...]
        hmid = _silu(_dot(x, wg_ref[...])) * _dot(x, wu_ref[...])
        y_ref[...] = _dot(hmid, wd_ref[...])

    @pl.when(pl.program_id(0) >= nu_ref[0])
    def _():
        y_ref[...] = jnp.zeros_like(y_ref)


def _experts(blk_e, n_used, x_pad, wg, wu, wd):
    n_rows, d = x_pad.shape
    ff = wg.shape[-1]
    n_blocks = n_rows // MOE_BLOCK
    rows = lambda i, be, nu: (jnp.minimum(i, nu[0] - 1), 0)
    return pl.pallas_call(
        _expert_kernel,
        grid_spec=pltpu.PrefetchScalarGridSpec(
            num_scalar_prefetch=2,
            grid=(n_blocks,),
            in_specs=[pl.BlockSpec((MOE_BLOCK, d), rows),
                      pl.BlockSpec((None, d, ff), lambda i, be, nu: (be[i], 0, 0)),
                      pl.BlockSpec((None, d, ff), lambda i, be, nu: (be[i], 0, 0)),
                      pl.BlockSpec((None, ff, d), lambda i, be, nu: (be[i], 0, 0))],
            out_specs=pl.BlockSpec((MOE_BLOCK, d), lambda i, be, nu: (i, 0)),
        ),
        out_shape=jax.ShapeDtypeStruct((n_rows, d), F32),
        compiler_params=_cparams(1),
        name="experts",
    )(blk_e, n_used, x_pad, wg, wu, wd)


def _combine_kernel(ps_ref, c1_ref, c2_ref, h2_ref, wt_ref, fnw_ref, y_ref, out_ref, b1_ref, b2_ref, sem, *, tr):
    base = pl.program_id(0) * tr

    def copy(d, buf, i):
        return pltpu.make_async_copy(y_ref.at[pl.ds(d, 1)], buf.at[pl.ds(i, 1)], sem)

    def issue(i, carry):
        copy(_dest(ps_ref, c1_ref[base + i]), b1_ref, i).start()
        copy(_dest(ps_ref, c2_ref[base + i]), b2_ref, i).start()
        return carry

    lax.fori_loop(0, tr, issue, 0)

    def drain(i, carry):
        copy(0, b1_ref, 0).wait()
        copy(0, b2_ref, 0).wait()
        return carry

    lax.fori_loop(0, tr, drain, 0)

    wt = wt_ref[...]
    h3 = h2_ref[...] + wt[:, 0:1] * b1_ref[...] + wt[:, 1:2] * b2_ref[...]
    out_ref[...] = _rms(h3) * fnw_ref[...]


def _combine(pstart, code1, code2, h2, wts, fnw, y_pad, tr):
    t, d = h2.shape
    return pl.pallas_call(
        functools.partial(_combine_kernel, tr=tr),
        grid_spec=pltpu.PrefetchScalarGridSpec(
            num_scalar_prefetch=3,
            grid=(t // tr,),
            in_specs=[pl.BlockSpec((tr, d), lambda i, *_: (i, 0)),
                      pl.BlockSpec((tr, LANES), lambda i, *_: (i, 0)),
                      pl.BlockSpec((1, d), lambda i, *_: (0, 0)),
                      pl.BlockSpec(memory_space=pl.ANY)],
            out_specs=pl.BlockSpec((tr, d), lambda i, *_: (i, 0)),
            scratch_shapes=[pltpu.VMEM((tr, d), F32), pltpu.VMEM((tr, d), F32), pltpu.SemaphoreType.DMA],
        ),
        out_shape=jax.ShapeDtypeStruct((t, d), F32),
        compiler_params=_cparams(1),
        name="combine",
    )(pstart, code1, code2, h2, wts, fnw, y_pad)


def _pick(n, pref):
    while n % pref:
        pref //= 2
    return pref


def kernel(x, mem, norm_mix_w, w_in, conv_w, gdn_a_log, gdn_dt_bias, gdn_out_norm_w, hgrn_lb, hgrn_out_norm_w, w_branch_a, w_branch_b, w_out, norm_xattn_w, norm_mem_w, xattn_wq, xattn_wkv, xattn_wo, norm_ffn_w, router_group_w, router_group_b, router_expert_w, router_expert_b, expert_w_gate, expert_w_up, expert_w_down, final_norm_w):
    b, s, d = x.shape
    t = b * s
    depth = w_in.shape[0]
    w = HEADS * HEAD_DIM
    qkv_w = 3 * w
    assert d == w and s % CHUNK == 0

    tc = _pick(s, 256)
    tm_proj = _pick(t, 1024)
    tm_post = _pick(s, 256)
    tr = _pick(t, 256)

    def pad_lanes(v, offset=0):
        return jnp.zeros((1, LANES), F32).at[0, offset:offset + v.shape[0]].set(v.astype(F32))

    assert depth == 1
    h3d = x
    for layer in range(depth):
        wl = w_in[layer]
        w_main = jnp.concatenate([wl[:, :qkv_w], wl[:, qkv_w + 2 * HEADS:]], axis=1).astype(BF16)
        w_small = jnp.zeros((d, LANES), F32).at[:, :2 * HEADS].set(wl[:, qkv_w:qkv_w + 2 * HEADS]).astype(BF16)
        p_main, p_small = _proj(h3d.reshape(t, d), norm_mix_w[layer][None, :], w_main, w_small,
                                tm_proj, 1024)
        p3 = p_main.reshape(b, s, -1)
        o_a = _gdn(p3, p_small.reshape(b, s, LANES), conv_w[layer], pad_lanes(gdn_a_log[layer]),
                   pad_lanes(gdn_dt_bias[layer]), tc)
        o_b = _hgrn(p3, hgrn_lb[layer:], tc)
        kv = _kv(mem, norm_mem_w[layer][None, :], xattn_wkv[layer].astype(BF16))
        w_router = jnp.zeros((d, LANES), F32)
        w_router = w_router.at[:, :N_GROUPS].set(router_group_w[layer])
        w_router = w_router.at[:, N_GROUPS:N_GROUPS + N_EXPERTS].set(router_expert_w[layer])
        b_router = pad_lanes(router_group_b[layer]) + pad_lanes(router_expert_b[layer], N_GROUPS)
        tile8 = lambda v: jnp.tile(v.astype(F32), HEADS)[None, :]
        h2, hn3, logits = _post(
            h3d, o_a, o_b, p3, kv, tile8(gdn_out_norm_w[layer]), tile8(hgrn_out_norm_w[layer]),
            w_branch_a[layer].astype(BF16), w_branch_b[layer].astype(BF16), w_out[layer].astype(BF16),
            norm_xattn_w[layer][None, :], xattn_wq[layer].astype(BF16), xattn_wo[layer].astype(BF16),
            norm_ffn_w[layer][None, :], w_router, b_router, tm_post)

        codes, wts, counts = _route(logits.reshape(t, LANES), tr)
        sizes = counts[0, :N_EXPERTS].astype(I32)
        padded = ((sizes + MOE_BLOCK - 1) // MOE_BLOCK) * MOE_BLOCK
        pend = jnp.cumsum(padded)
        pstart = (pend - padded).astype(I32)
        m = t * 2
        n_rows = ((m + MOE_BLOCK - 1) // MOE_BLOCK) * MOE_BLOCK + N_EXPERTS * MOE_BLOCK
        n_blocks = n_rows // MOE_BLOCK
        blk_start = jnp.arange(n_blocks, dtype=I32) * MOE_BLOCK
        blk_e = jnp.minimum(jnp.sum(pend[None, :] <= blk_start[:, None], axis=1), N_EXPERTS - 1).astype(I32)
        n_used = (pend[-1:] // MOE_BLOCK).astype(I32)
        code1, code2 = codes[:, 0], codes[:, 1]

        x_pad = _dispatch(pstart, code1, code2, hn3.reshape(t, d), n_rows, tr)
        y_pad = _experts(blk_e, n_used, x_pad, expert_w_gate[layer], expert_w_up[layer], expert_w_down[layer])
        out =_combine(pstart, code1, code2, h2.reshape(t, d), wts, final_norm_w[None, :], y_pad, tr)
        h3d = out.reshape(b, s, d)
    return h3d
```

```python
import functools

import jax
import jax.numpy as jnp
from jax import lax
from jax.experimental import pallas as pl
from jax.experimental.pallas import tpu as pltpu

F32 = jnp.float32
BF16 = jnp.bfloat16
I32 = jnp.int32

EPS = 1e-6
CHUNK = 64
HEADS = 8
HEAD_DIM = 128
CONV_K = 4
XA_HEADS = 4
N_GROUPS = 4
EXP_PER_GROUP = 8
N_EXPERTS = N_GROUPS * EXP_PER_GROUP
MOE_BLOCK = 256
LANES = 128
RANK_BITS = 20

VMEM_LIMIT = 52 * 1024 * 1024


def _cparams(n_axes):
    return pltpu.CompilerParams(dimension_semantics=("arbitrary",) * n_axes,
                                vmem_limit_bytes=VMEM_LIMIT)


def _dot(a, b):
    return jnp.dot(a.astype(BF16), b.astype(BF16), preferred_element_type=F32)


def _dot_nt(a, b):
    return lax.dot_general(a.astype(BF16), b.astype(BF16), (((1,), (1,)), ((), ())),
                           preferred_element_type=F32)


def _dot_tn(a, b):
    return lax.dot_general(a.astype(BF16), b.astype(BF16), (((0,), (0,)), ((), ())),
                           preferred_element_type=F32)


def _split(x):
    hi = x.astype(BF16)
    lo = (x - hi.astype(F32)).astype(BF16)
    return hi, lo


def _dot_exact_lhs(m_bf16, x):
    hi, lo = _split(x)
    return (jnp.dot(m_bf16, hi, preferred_element_type=F32)
            + jnp.dot(m_bf16, lo, preferred_element_type=F32))


def _rms(x):
    return x * lax.rsqrt(jnp.mean(x * x, axis=-1, keepdims=True) + EPS)


def _silu(x):
    return x * jax.nn.sigmoid(x)


def _softplus(x):
    return jnp.maximum(x, 0.0) + jnp.log(1.0 + jnp.exp(-jnp.abs(x)))


def _to_slab(ref, x):
    n, d = x.shape
    ns = d // LANES
    for s in range(ns):
        ref[pl.ds(s, n, stride=ns), :] = x[:, s * LANES:(s + 1) * LANES]


def _from_slab(ref, n):
    ns = ref.shape[0] // n
    return jnp.concatenate([ref[pl.ds(s, n, stride=ns), :] for s in range(ns)], axis=1)


def _chunk_masks(tc):
    ri = lax.broadcasted_iota(I32, (tc, tc), 0)
    ci = lax.broadcasted_iota(I32, (tc, tc), 1)
    same = (ri // CHUNK) == (ci // CHUNK)
    causal = same & (ri >= ci)
    strict = same & (ri > ci)
    return ri, ci, causal, strict


def _proj_kernel(x_ref, nw_ref, w_ref, ws_ref, out_ref, small_ref, hn_ref):
    @pl.when(pl.program_id(1) == 0)
    def _():
        hn = (_rms(x_ref[...]) * nw_ref[...]).astype(BF16)
        hn_ref[...] = hn
        small_ref[...] = jnp.dot(hn, ws_ref[...], preferred_element_type=F32)

    out_ref[...] = jnp.dot(hn_ref[...], w_ref[...], preferred_element_type=F32).astype(BF16)


def _proj(x2, nw, w_main, w_small, tm, tn):
    t, d = x2.shape
    n = w_main.shape[1]
    return pl.pallas_call(
        _proj_kernel,
        grid=(t // tm, n // tn),
        in_specs=[
            pl.BlockSpec((tm, d), lambda i, j: (i, 0)),
            pl.BlockSpec((1, d), lambda i, j: (0, 0)),
            pl.BlockSpec((d, tn), lambda i, j: (0, j)),
            pl.BlockSpec((d, LANES), lambda i, j: (0, 0)),
        ],
        out_specs=[
            pl.BlockSpec((tm, tn), lambda i, j: (i, j)),
            pl.BlockSpec((tm, LANES), lambda i, j: (i, 0)),
        ],
        out_shape=[jax.ShapeDtypeStruct((t, n), BF16), jax.ShapeDtypeStruct((t, LANES), F32)],
        scratch_shapes=[pltpu.VMEM((tm, d), BF16)],
        compiler_params=_cparams(2),
        name="proj",
    )(x2, nw, w_main, w_small)


def _gdn_kernel(q_ref, k_ref, v_ref, sm_ref, cw_ref, alog_ref, dtb_ref, o_ref, xs_ref, st_ref, *, tc):
    w = HEADS * HEAD_DIM
    nc = tc // CHUNK

    @pl.when(pl.program_id(1) == 0)
    def _():
        xs_ref[0:8, :] = jnp.zeros((8, 3 * w), F32)
        st_ref[...] = jnp.zeros_like(st_ref)

    xs_ref[8:8 + tc, 0:w] = q_ref[...].astype(F32)
    xs_ref[8:8 + tc, w:2 * w] = k_ref[...].astype(F32)
    xs_ref[8:8 + tc, 2 * w:3 * w] = v_ref[...].astype(F32)

    _, _, causal, strict = _chunk_masks(tc)
    tri = jnp.where(causal, 1.0, 0.0).astype(BF16)
    wi = lax.broadcasted_iota(I32, (CHUNK, tc), 0)
    wj = lax.broadcasted_iota(I32, (CHUNK, tc), 1)
    eye_w = jnp.where(wi == wj % CHUNK, 1.0, 0.0)
    blk_w = wj // CHUNK

    def fold(m_bd):
        acc = m_bd[0:CHUNK]
        for c in range(1, nc):
            acc = acc + m_bd[c * CHUNK:(c + 1) * CHUNK]
        return acc

    def unfold(m_w):
        return jnp.concatenate([jnp.where(blk_w == c, m_w, 0.0) for c in range(nc)], axis=0)

    sm = sm_ref[...]
    lane = lax.broadcasted_iota(I32, (tc, LANES), 1)
    g_all = jnp.where(lane < HEADS, -jnp.exp(alog_ref[...]) * _softplus(sm + dtb_ref[...]), 0.0)
    beta_all = jax.nn.sigmoid(sm)
    cum = _dot_exact_lhs(tri, g_all)
    ecum = jnp.exp(cum)
    cum_t = cum.T

    def conv_part(col):
        acc = cw_ref[CONV_K - 1:CONV_K, col:col + HEAD_DIM] * xs_ref[8:8 + tc, col:col + HEAD_DIM]
        for j in range(CONV_K - 1):
            off = 8 - (CONV_K - 1) + j
            acc = acc + cw_ref[j:j + 1, col:col + HEAD_DIM] * xs_ref[off:off + tc, col:col + HEAD_DIM]
        return _silu(acc)

    def l2n(x):
        return x * lax.rsqrt(jnp.sum(x * x, axis=-1, keepdims=True) + EPS)

    heads = range(HEADS)
    ks, a_bd, qk_bd, rhs, qd, cc = [], [], [], [], [], []
    for h in heads:
        q = l2n(conv_part(h * HEAD_DIM)) * HEAD_DIM ** -0.5
        k = l2n(conv_part(w + h * HEAD_DIM))
        v = conv_part(2 * w + h * HEAD_DIM)
        cch = cum[:, h:h + 1]
        cr = cum_t[h:h + 1, :]
        bc = beta_all[:, HEADS + h:HEADS + h + 1]
        ec = ecum[:, h:h + 1]
        dec = jnp.where(causal, jnp.exp(jnp.where(causal, cch - cr, 0.0)), 0.0)
        kb = k * bc
        a_bd.append(jnp.where(strict, _dot_nt(kb, k) * dec, 0.0))
        qk_bd.append(_dot_nt(q, k) * dec)
        rhs.append(jnp.concatenate([v * bc, kb * ec], axis=1).astype(BF16))
        qd.append(q * ec)
        ks.append(k)
        cc.append(cch)

    xw, pw = [], []
    for h in heads:
        aw = fold(a_bd[h])
        xw.append(eye_w - aw)
        pw.append(_dot(aw, a_bd[h]))
    n_sq = CHUNK.bit_length() - 2
    for it in range(n_sq):
        for h in heads:
            p_bd = unfold(pw[h]).astype(BF16)
            if it + 1 < n_sq:
                res = _dot(jnp.concatenate([xw[h], pw[h]], axis=0), p_bd)
                xw[h] = xw[h] + res[:CHUNK]
                pw[h] = res[CHUNK:]
            else:
                xw[h] = xw[h] + _dot(xw[h], p_bd)

    uw, qp, o0 = [], [], []
    for h in heads:
        uwh = _dot(unfold(xw[h]), rhs[h])
        qkuw = _dot(qk_bd[h], uwh)
        uw.append(uwh.astype(BF16))
        o0.append(qkuw[:, :HEAD_DIM])
        qp.append(qd[h] - qkuw[:, HEAD_DIM:])

    state = [st_ref[h] for h in heads]
    for c in range(nc):
        r0 = c * CHUNK
        rows = slice(r0, r0 + CHUNK)
        kuw, dlast = [], []
        for h in heads:
            last = cc[h][r0 + CHUNK - 1:r0 + CHUNK, :]
            kd = ks[h][rows] * jnp.exp(last - cc[h][rows])
            kuw.append(_dot_tn(kd, uw[h][rows]))
            dlast.append(jnp.exp(last))
        for h in heads:
            s = state[h]
            res = _dot(jnp.concatenate([kuw[h][:, HEAD_DIM:], qp[h][rows]], axis=0), s)
            o = res[HEAD_DIM:] + o0[h][rows]
            state[h] = s * dlast[h] - res[:HEAD_DIM] + kuw[h][:, :HEAD_DIM]
            o_ref[rows, h * HEAD_DIM:(h + 1) * HEAD_DIM] = _rms(o).astype(BF16)
    for h in heads:
        st_ref[h] = state[h]

    xs_ref[0:8, :] = xs_ref[tc:tc + 8, :]


def _gdn(p3, small3, conv_w, alog, dtb, tc):
    b, s, _ = p3.shape
    w = HEADS * HEAD_DIM
    return pl.pallas_call(
        functools.partial(_gdn_kernel, tc=tc),
        grid=(b, s // tc),
        in_specs=[
            pl.BlockSpec((None, tc, w), lambda i, j: (i, j, 0)),
            pl.BlockSpec((None, tc, w), lambda i, j: (i, j, 1)),
            pl.BlockSpec((None, tc, w), lambda i, j: (i, j, 2)),
            pl.BlockSpec((None, tc, LANES), lambda i, j: (i, j, 0)),
            pl.BlockSpec((CONV_K, 3 * w), lambda i, j: (0, 0)),
            pl.BlockSpec((1, LANES), lambda i, j: (0, 0)),
            pl.BlockSpec((1, LANES), lambda i, j: (0, 0)),
        ],
        out_specs=pl.BlockSpec((None, tc, w), lambda i, j: (i, j, 0)),
        out_shape=jax.ShapeDtypeStruct((b, s, w), BF16),
        scratch_shapes=[pltpu.VMEM((tc + 8, 3 * w), F32), pltpu.VMEM((HEADS, HEAD_DIM, HEAD_DIM), F32)],
        compiler_params=_cparams(2),
        name="gdn",
    )(p3, p3, p3, small3, conv_w, alog, dtb)


def _hgrn_kernel(f_ref, q_ref, i_ref, lb_ref, o_ref, st_ref, *, tc):
    nc = tc // CHUNK

    @pl.when(pl.program_id(1) == 0)
    def _():
        st_ref[...] = jnp.zeros_like(st_ref)

    _, _, causal, _ = _chunk_masks(tc)
    tri = jnp.where(causal, 1.0, 0.0).astype(BF16)

    lbp = lb_ref[...]
    lbe = jnp.exp(lbp - jnp.max(lbp, axis=0, keepdims=True))
    lb = lbe[0:1, :] / jnp.sum(lbe, axis=0, keepdims=True)

    heads = range(HEADS)
    hcols = [slice(h * HEAD_DIM, (h + 1) * HEAD_DIM) for h in heads]
    forget = lb + (1.0 - lb) * jax.nn.sigmoid(f_ref[...].astype(F32))
    cum = _dot_exact_lhs(tri, jnp.log(forget))
    kk = 1.0 - forget
    q_in = (q_ref[...].astype(F32) * HEAD_DIM ** -0.5 * jnp.exp(cum)).astype(BF16)
    k_in = (kk * jnp.exp(-cum)).astype(BF16)
    v = i_ref[...]
    intra = [jnp.where(causal, _dot_nt(q_in[:, hc], k_in[:, hc]), 0.0).astype(BF16) for hc in hcols]
    state = [st_ref[h] for h in heads]
    for c in range(nc):
        r0 = c * CHUNK
        rows = slice(r0, r0 + CHUNK)
        last = cum[r0 + CHUNK - 1:r0 + CHUNK, :]
        k_dec = (kk[rows] * jnp.exp(last - cum[rows])).astype(BF16)
        dlast = jnp.exp(last)
        for h, hc in zip(heads, hcols):
            o = _dot_nt(q_in[rows, hc], state[h]) + _dot(intra[h][rows, rows], v[rows, hc])
            state[h] = state[h] * dlast[:, hc] + _dot_tn(v[rows, hc], k_dec[:, hc])
            o_ref[rows, hc] = _rms(o).astype(BF16)
    for h in heads:
        st_ref[h] = state[h]


def _hgrn(p3, lb_logits, tc):
    b, s, _ = p3.shape
    w = HEADS * HEAD_DIM
    return pl.pallas_call(
        functools.partial(_hgrn_kernel, tc=tc),
        grid=(b, s // tc),
        in_specs=[
            pl.BlockSpec((None, tc, w), lambda i, j: (i, j, 4)),
            pl.BlockSpec((None, tc, w), lambda i, j: (i, j, 5)),
            pl.BlockSpec((None, tc, w), lambda i, j: (i, j, 6)),
            pl.BlockSpec(lb_logits.shape, lambda i, j: (0, 0)),
        ],
        out_specs=pl.BlockSpec((None, tc, w), lambda i, j: (i, j, 0)),
        out_shape=jax.ShapeDtypeStruct((b, s, w), BF16),
        scratch_shapes=[pltpu.VMEM((HEADS, HEAD_DIM, HEAD_DIM), F32)],
        compiler_params=_cparams(2),
        name="hgrn",
    )(p3, p3, p3, lb_logits)


def _kv_kernel(mem_ref, nw_ref, wkv_ref, kv_ref):
    mn = (_rms(mem_ref[...]) * nw_ref[...]).astype(BF16)
    kv_ref[...] = jnp.dot(mn, wkv_ref[...], preferred_element_type=F32).astype(BF16)


def _kv(mem, nw, wkv):
    b, m, d = mem.shape
    return pl.pallas_call(
        _kv_kernel,
        grid=(b,),
        in_specs=[
            pl.BlockSpec((None, m, d), lambda i: (i, 0, 0)),
            pl.BlockSpec((1, d), lambda i: (0, 0)),
            pl.BlockSpec(wkv.shape, lambda i: (0, 0)),
        ],
        out_specs=pl.BlockSpec((None, m, 2 * d), lambda i: (i, 0, 0)),
        out_shape=jax.ShapeDtypeStruct((b, m, 2 * d), BF16),
        compiler_params=_cparams(1),
        name="kv",
    )(mem, nw, wkv)


def _post_kernel(x_ref, oa_ref, ob_ref, oga_ref, ogb_ref, ga_ref, gb_ref, kv_ref,
                 gnw_ref, hnw_ref, wa_ref, wb_ref, wout_ref, nx_ref, wq_ref, wo_ref, nf_ref,
                 wr_ref, br_ref, h2_ref, hn3_ref, lg_ref):
    d = x_ref.shape[-1]
    dh = d // XA_HEADS
    ya = oa_ref[...].astype(F32) * gnw_ref[...] * _silu(oga_ref[...].astype(F32))
    yb = ob_ref[...].astype(F32) * hnw_ref[...] * _silu(ogb_ref[...].astype(F32))
    merged = (jax.nn.sigmoid(ga_ref[...].astype(F32)) * _dot(ya, wa_ref[...])
              + jax.nn.sigmoid(gb_ref[...].astype(F32)) * _dot(yb, wb_ref[...]))
    h1 = x_ref[...] + _dot(merged, wout_ref[...])

    q = _dot(_rms(h1) * nx_ref[...], wq_ref[...]) * dh ** -0.5
    outs = []
    for hh in range(XA_HEADS):
        kh = kv_ref[:, hh * dh:(hh + 1) * dh]
        vh = kv_ref[:, d + hh * dh:d + (hh + 1) * dh]
        sc = _dot_nt(q[:, hh * dh:(hh + 1) * dh], kh)
        p = jnp.exp(sc - jnp.max(sc, axis=-1, keepdims=True))
        outs.append(_dot(p, vh) / jnp.sum(p, axis=-1, keepdims=True))
    h2 = h1 + _dot(jnp.concatenate(outs, axis=1), wo_ref[...])
    h2_ref[...] = h2

    hn3 = _rms(h2) * nf_ref[...]
    _to_slab(hn3_ref, hn3)
    hi, lo = _split(hn3)
    whi, wlo = _split(wr_ref[...])
    lg_ref[...] = (jnp.dot(hi, whi, preferred_element_type=F32)
                   + jnp.dot(hi, wlo, preferred_element_type=F32)
                   + jnp.dot(lo, whi, preferred_element_type=F32)) + br_ref[...]


def _post(x3, oa, ob, p3, kv, gnw, hnw, wa, wb, wout, nx, wq, wo, nf, wr, br, tm):
    b, s, d = x3.shape
    row = lambda c: pl.BlockSpec((None, tm, d), lambda i, j: (i, j, c))
    full = lambda a: pl.BlockSpec(a.shape, lambda i, j: (0,) * a.ndim)
    return pl.pallas_call(
        _post_kernel,
        grid=(b, s // tm),
        in_specs=[row(0), row(0), row(0), row(3), row(7), row(8), row(9),
                  pl.BlockSpec((None,) + kv.shape[1:], lambda i, j: (i, 0, 0)),
                  full(gnw), full(hnw), full(wa), full(wb), full(wout), full(nx), full(wq), full(wo),
                  full(nf), full(wr), full(br)],
        out_specs=[row(0), pl.BlockSpec((tm * (d // LANES), LANES), lambda i, j: (i * (s // tm) + j, 0)),
                   pl.BlockSpec((None, tm, LANES), lambda i, j: (i, j, 0))],
        out_shape=[jax.ShapeDtypeStruct((b, s, d), F32), jax.ShapeDtypeStruct((b * s * (d // LANES), LANES), F32),
                   jax.ShapeDtypeStruct((b, s, LANES), F32)],
        compiler_params=_cparams(2),
        name="post",
    )(x3, oa, ob, p3, p3, p3, p3, kv, gnw, hnw, wa, wb, wout, nx, wq, wo, nf, wr, br)


def _route_kernel(lg_ref, code_ref, wt_ref, cnt_ref, carry_ref, *, tr):
    @pl.when(pl.program_id(0) == 0)
    def _():
        carry_ref[...] = jnp.zeros_like(carry_ref)

    neg = -1e30
    big = 2 * LANES
    lg = lg_ref[...]
    lane = lax.broadcasted_iota(I32, (tr, LANES), 1)
    lane_f = lane.astype(F32)
    first = lambda m: jnp.min(jnp.where(m, lane_f, big), axis=-1, keepdims=True).astype(I32)

    is_g = lane < N_GROUPS
    gl = jnp.where(is_g, lg, neg)
    gmax = jnp.max(gl, axis=-1, keepdims=True)
    gidx = first(gl == gmax)
    g_p = 1.0 / jnp.sum(jnp.where(is_g, jnp.exp(gl - gmax), 0.0), axis=-1, keepdims=True)

    lo_lane = N_GROUPS + gidx * EXP_PER_GROUP
    in_grp = (lane >= lo_lane) & (lane < lo_lane + EXP_PER_GROUP)
    el = jnp.where(in_grp, lg, neg)
    m1 = jnp.max(el, axis=-1, keepdims=True)
    i1 = first(el == m1)
    el2 = jnp.where(lane == i1, neg, el)
    m2 = jnp.max(el2, axis=-1, keepdims=True)
    i2 = first(el2 == m2)
    esum = jnp.sum(jnp.where(in_grp, jnp.exp(el - m1), 0.0), axis=-1, keepdims=True)
    p1 = 1.0 / esum
    p2 = jnp.exp(m2 - m1) / esum
    w1 = g_p * p1 / (p1 + p2)
    w2 = g_p * p2 / (p1 + p2)
    e1 = i1 - N_GROUPS
    e2 = i2 - N_GROUPS

    oh1 = lane == e1
    oh2 = lane == e2
    both = jnp.where(oh1 | oh2, 1.0, 0.0)
    ri = lax.broadcasted_iota(I32, (tr, tr), 0)
    ci = lax.broadcasted_iota(I32, (tr, tr), 1)
    below = jnp.where(ri > ci, 1.0, 0.0).astype(BF16)
    before = jnp.dot(below, both.astype(BF16), preferred_element_type=F32) + carry_ref[...]
    r1 = jnp.sum(jnp.where(oh1, before, 0.0), axis=-1, keepdims=True).astype(I32)
    r2 = jnp.sum(jnp.where(oh2, before, 0.0), axis=-1, keepdims=True).astype(I32)
    carry_ref[...] = carry_ref[...] + jnp.sum(both, axis=0, keepdims=True)

    code_ref[...] = jnp.where(lane == 0, e1 * (1 << RANK_BITS) + r1,
                              jnp.where(lane == 1, e2 * (1 << RANK_BITS) + r2, 0))
    wt_ref[...] = jnp.where(lane == 0, w1, jnp.where(lane == 1, w2, 0.0))
    cnt_ref[...] = carry_ref[...]


def _route(logits, tr):
    t = logits.shape[0]
    return pl.pallas_call(
        functools.partial(_route_kernel, tr=tr),
        grid=(t // tr,),
        in_specs=[pl.BlockSpec((tr, LANES), lambda i: (i, 0))],
        out_specs=[pl.BlockSpec((tr, LANES), lambda i: (i, 0)),
                   pl.BlockSpec((tr, LANES), lambda i: (i, 0)),
                   pl.BlockSpec((1, LANES), lambda i: (0, 0))],
        out_shape=[jax.ShapeDtypeStruct((t, LANES), I32), jax.ShapeDtypeStruct((t, LANES), F32),
                   jax.ShapeDtypeStruct((1, LANES), F32)],
        scratch_shapes=[pltpu.VMEM((1, LANES), F32)],
        compiler_params=_cparams(1),
        name="route",
    )(logits)


def _dest(ps_ref, code):
    return ps_ref[code >> RANK_BITS] + (code & ((1 << RANK_BITS) - 1))


ISSUE_UNROLL = 8


def _slab_rows(ref, row, ns):
    return ref.at[pl.ds(pl.multiple_of(row * ns, ns), ns)]


def _dispatch_kernel(ps_ref, sz_ref, c1_ref, c2_ref, hn_ref, xp_ref, zero_ref, sem, *, tr, ns):
    step = pl.program_id(0)
    base = step * tr

    def copy(i, d):
        return pltpu.make_async_copy(_slab_rows(hn_ref, i, ns), _slab_rows(xp_ref, d, ns), sem)

    def pad_copy(d):
        return pltpu.make_async_copy(zero_ref.at[pl.ds(0, ns)], _slab_rows(xp_ref, d, ns), sem)

    def pad_block_copy(blk):
        return pltpu.make_async_copy(zero_ref, _slab_rows(xp_ref, blk, MOE_BLOCK * ns), sem)

    def issue(g, carry):
        for u in range(ISSUE_UNROLL):
            i = g * ISSUE_UNROLL + u
            copy(i, _dest(ps_ref, c1_ref[base + i])).start()
            copy(i, _dest(ps_ref, c2_ref[base + i])).start()
        return carry

    def drain(g, carry):
        for _ in range(2 * ISSUE_UNROLL):
            copy(0, 0).wait()
        return carry

    lax.fori_loop(0, tr // ISSUE_UNROLL, issue, 0)
    lax.fori_loop(0, tr // ISSUE_UNROLL, drain, 0)

    @pl.when(step == pl.num_programs(0) - 1)
    def _():
        zero_ref[...] = jnp.zeros_like(zero_ref)

        def pad_expert(e, n_pad):
            n = sz_ref[e]
            first = ps_ref[e] + n
            n_e = (MOE_BLOCK - n % MOE_BLOCK) % MOE_BLOCK

            def one(r, c):
                pad_copy(first + r).start()
                return c

            lax.fori_loop(0, n_e, one, 0)
            return n_pad + n_e

        n_pad = lax.fori_loop(0, N_EXPERTS, pad_expert, 0)

        def drain_pad(r, c):
            pad_copy(0).wait()
            return c

        lax.fori_loop(0, n_pad, drain_pad, 0)

        first_blk = (ps_ref[N_EXPERTS - 1] + sz_ref[N_EXPERTS - 1] + MOE_BLOCK - 1) // MOE_BLOCK
        n_blocks = xp_ref.shape[0] // (MOE_BLOCK * ns)

        def tail_start(blk, c):
            pad_block_copy(blk).start()
            return c

        def tail_wait(blk, c):
            pad_block_copy(blk).wait()
            return c

        lax.fori_loop(first_blk, n_blocks, tail_start, 0)
        lax.fori_loop(first_blk, n_blocks, tail_wait, 0)


def _dispatch(pstart, sizes, code1, code2, hn3_slab, n_rows, tr, ns):
    t = code1.shape[0]
    return pl.pallas_call(
        functools.partial(_dispatch_kernel, tr=tr, ns=ns),
        grid_spec=pltpu.PrefetchScalarGridSpec(
            num_scalar_prefetch=4,
            grid=(t // tr,),
            in_specs=[pl.BlockSpec((tr * ns, LANES), lambda i, *_: (i, 0))],
            out_specs=pl.BlockSpec(memory_space=pl.ANY),
            scratch_shapes=[pltpu.VMEM((MOE_BLOCK * ns, LANES), F32), pltpu.SemaphoreType.DMA],
        ),
        out_shape=jax.ShapeDtypeStruct((n_rows * ns, LANES), F32),
        compiler_params=_cparams(1),
        name="dispatch",
    )(pstart, sizes, code1, code2, hn3_slab)


def _expert_kernel(be_ref, nu_ref, x_ref, wg_ref, wu_ref, wd_ref, y_ref):
    del be_ref

    @pl.when(pl.program_id(0) < nu_ref[0])
    def _():
        x = _from_slab(x_ref, MOE_BLOCK).astype(BF16)
        hmid = _silu(_dot(x, wg_ref[...])) * _dot(x, wu_ref[...])
        _to_slab(y_ref, _dot(hmid, wd_ref[...]))

    @pl.when(pl.program_id(0) >= nu_ref[0])
    def _():
        y_ref[...] = jnp.zeros_like(y_ref)


def _experts(blk_e, n_used, x_pad, wg, wu, wd):
    d, ff = wg.shape[-2:]
    blk = MOE_BLOCK * (d // LANES)
    n_blocks = x_pad.shape[0] // blk
    rows = lambda i, be, nu: (jnp.minimum(i, nu[0] - 1), 0)
    return pl.pallas_call(
        _expert_kernel,
        grid_spec=pltpu.PrefetchScalarGridSpec(
            num_scalar_prefetch=2,
            grid=(n_blocks,),
            in_specs=[pl.BlockSpec((blk, LANES), rows),
                      pl.BlockSpec((None, d, ff), lambda i, be, nu: (be[i], 0, 0)),
                      pl.BlockSpec((None, d, ff), lambda i, be, nu: (be[i], 0, 0)),
                      pl.BlockSpec((None, ff, d), lambda i, be, nu: (be[i], 0, 0))],
            out_specs=pl.BlockSpec((blk, LANES), lambda i, be, nu: (i, 0)),
        ),
        out_shape=jax.ShapeDtypeStruct(x_pad.shape, F32),
        compiler_params=_cparams(1),
        name="experts",
    )(blk_e, n_used, x_pad, wg, wu, wd)


def _combine_kernel(ps_ref, c1_ref, c2_ref, h2_ref, wt_ref, fnw_ref, y_ref, out_ref, b1_ref, b2_ref, sem, *, tr, ns):
    step = pl.program_id(0)
    n_steps = pl.num_programs(0)

    def copy(d, buf, slot, i):
        return pltpu.make_async_copy(_slab_rows(y_ref, d, ns), _slab_rows(buf.at[slot], i, ns), sem.at[slot])

    def issue_step(st):
        slot = st % 2

        def issue(g, carry):
            for u in range(ISSUE_UNROLL):
                i = g * ISSUE_UNROLL + u
                copy(_dest(ps_ref, c1_ref[st * tr + i]), b1_ref, slot, i).start()
                copy(_dest(ps_ref, c2_ref[st * tr + i]), b2_ref, slot, i).start()
            return carry

        lax.fori_loop(0, tr // ISSUE_UNROLL, issue, 0)

    @pl.when(step == 0)
    def _():
        issue_step(step)

    @pl.when(step + 1 < n_steps)
    def _():
        issue_step(step + 1)

    slot = step % 2

    def drain(g, carry):
        for _ in range(ISSUE_UNROLL):
            copy(0, b1_ref, slot, 0).wait()
            copy(0, b2_ref, slot, 0).wait()
        return carry

    lax.fori_loop(0, tr // ISSUE_UNROLL, drain, 0)

    wt = wt_ref[...]
    h3 = (h2_ref[...] + wt[:, 0:1] * _from_slab(b1_ref.at[slot], tr)
          + wt[:, 1:2] * _from_slab(b2_ref.at[slot], tr))
    out_ref[...] = _rms(h3) * fnw_ref[...]


def _combine(pstart, code1, code2, h2, wts, fnw, y_pad, tr):
    t, d = h2.shape
    ns = d // LANES
    return pl.pallas_call(
        functools.partial(_combine_kernel, tr=tr, ns=ns),
        grid_spec=pltpu.PrefetchScalarGridSpec(
            num_scalar_prefetch=3,
            grid=(t // tr,),
            in_specs=[pl.BlockSpec((tr, d), lambda i, *_: (i, 0)),
                      pl.BlockSpec((tr, LANES), lambda i, *_: (i, 0)),
                      pl.BlockSpec((1, d), lambda i, *_: (0, 0)),
                      pl.BlockSpec(memory_space=pl.ANY)],
            out_specs=pl.BlockSpec((tr, d), lambda i, *_: (i, 0)),
            scratch_shapes=[pltpu.VMEM((2, tr * ns, LANES), F32), pltpu.VMEM((2, tr * ns, LANES), F32),
                            pltpu.SemaphoreType.DMA((2,))],
        ),
        out_shape=jax.ShapeDtypeStruct((t, d), F32),
        compiler_params=_cparams(1),
        name="combine",
    )(pstart, code1, code2, h2, wts, fnw, y_pad)


def _pick(n, pref):
    while n % pref:
        pref //= 2
    return pref


def kernel(x, mem, norm_mix_w, w_in, conv_w, gdn_a_log, gdn_dt_bias, gdn_out_norm_w, hgrn_lb, hgrn_out_norm_w, w_branch_a, w_branch_b, w_out, norm_xattn_w, norm_mem_w, xattn_wq, xattn_wkv, xattn_wo, norm_ffn_w, router_group_w, router_group_b, router_expert_w, router_expert_b, expert_w_gate, expert_w_up, expert_w_down, final_norm_w):
    b, s, d = x.shape
    t = b * s
    depth = w_in.shape[0]
    w = HEADS * HEAD_DIM
    qkv_w = 3 * w
    assert d == w and s % CHUNK == 0

    tc = _pick(s, 256)
    tm_proj = _pick(t, 1024)
    tm_post = _pick(s, 256)
    tr = _pick(t, 256)

    def pad_lanes(v, offset=0):
        return jnp.zeros((1, LANES), F32).at[0, offset:offset + v.shape[0]].set(v.astype(F32))

    assert depth == 1
    h3d = x
    for layer in range(depth):
        wl = w_in[layer]
        w_main = jnp.concatenate([wl[:, :qkv_w], wl[:, qkv_w + 2 * HEADS:]], axis=1).astype(BF16)
        w_small = jnp.zeros((d, LANES), F32).at[:, :2 * HEADS].set(wl[:, qkv_w:qkv_w + 2 * HEADS]).astype(BF16)
        p_main, p_small = _proj(h3d.reshape(t, d), norm_mix_w[layer][None, :], w_main, w_small,
                                tm_proj, 1024)
        p3 = p_main.reshape(b, s, -1)
        o_a = _gdn(p3, p_small.reshape(b, s, LANES), conv_w[layer], pad_lanes(gdn_a_log[layer]),
                   pad_lanes(gdn_dt_bias[layer]), tc)
        o_b = _hgrn(p3, hgrn_lb[layer:], tc)
        kv = _kv(mem, norm_mem_w[layer][None, :], xattn_wkv[layer].astype(BF16))
        w_router = jnp.zeros((d, LANES), F32)
        w_router = w_router.at[:, :N_GROUPS].set(router_group_w[layer])
        w_router = w_router.at[:, N_GROUPS:N_GROUPS + N_EXPERTS].set(router_expert_w[layer])
        b_router = pad_lanes(router_group_b[layer]) + pad_lanes(router_expert_b[layer], N_GROUPS)
        tile8 = lambda v: jnp.tile(v.astype(F32), HEADS)[None, :]
        h2, hn3, logits = _post(
            h3d, o_a, o_b, p3, kv, tile8(gdn_out_norm_w[layer]), tile8(hgrn_out_norm_w[layer]),
            w_branch_a[layer].astype(BF16), w_branch_b[layer].astype(BF16), w_out[layer].astype(BF16),
            norm_xattn_w[layer][None, :], xattn_wq[layer].astype(BF16), xattn_wo[layer].astype(BF16),
            norm_ffn_w[layer][None, :], w_router, b_router, tm_post)

        codes, wts, counts = _route(logits.reshape(t, LANES), tr)
        sizes = counts[0, :N_EXPERTS].astype(I32)
        padded = ((sizes + MOE_BLOCK - 1) // MOE_BLOCK) * MOE_BLOCK
        pend = jnp.cumsum(padded)
        pstart = (pend - padded).astype(I32)
        m = t * 2
        n_rows = ((m + MOE_BLOCK - 1) // MOE_BLOCK) * MOE_BLOCK + N_EXPERTS * MOE_BLOCK
        n_blocks = n_rows // MOE_BLOCK
        blk_start = jnp.arange(n_blocks, dtype=I32) * MOE_BLOCK
        blk_e = jnp.minimum(jnp.sum(pend[None, :] <= blk_start[:, None], axis=1), N_EXPERTS - 1).astype(I32)
        n_used = (pend[-1:] // MOE_BLOCK).astype(I32)
        code1, code2 = codes[:, 0], codes[:, 1]

        x_pad = _dispatch(pstart, sizes, code1, code2, hn3, n_rows, _pick(t, 1024), d // LANES)
        y_pad = _experts(blk_e, n_used, x_pad, expert_w_gate[layer], expert_w_up[layer], expert_w_down[layer])
        out =_combine(pstart, code1, code2, h2.reshape(t, d), wts, final_norm_w[None, :], y_pad, tr)
        h3d = out.reshape(b, s, d)
    return h3d
```

```python
import functools

import jax
import jax.numpy as jnp
from jax import lax
from jax.experimental import pallas as pl
from jax.experimental.pallas import tpu as pltpu

F32 = jnp.float32
BF16 = jnp.bfloat16
I32 = jnp.int32
U32 = jnp.uint32

EPS = 1e-6
CHUNK = 64
HEADS = 8
HEAD_DIM = 128
CONV_K = 4
XA_HEADS = 4
N_GROUPS = 4
EXP_PER_GROUP = 8
N_EXPERTS = N_GROUPS * EXP_PER_GROUP
MOE_BLOCK = 256
LANES = 128
RANK_BITS = 20

VMEM_LIMIT = 52 * 1024 * 1024


def _cparams(n_axes):
    return pltpu.CompilerParams(dimension_semantics=("arbitrary",) * n_axes,
                                vmem_limit_bytes=VMEM_LIMIT)


def _dot(a, b):
    return jnp.dot(a.astype(BF16), b.astype(BF16), preferred_element_type=F32)


def _dot_nt(a, b):
    return lax.dot_general(a.astype(BF16), b.astype(BF16), (((1,), (1,)), ((), ())),
                           preferred_element_type=F32)


def _dot_tn(a, b):
    return lax.dot_general(a.astype(BF16), b.astype(BF16), (((0,), (0,)), ((), ())),
                           preferred_element_type=F32)


def _split(x):
    hi = x.astype(BF16)
    lo = (x - hi.astype(F32)).astype(BF16)
    return hi, lo


def _dot_exact_lhs(m_bf16, x):
    hi, lo = _split(x)
    return (jnp.dot(m_bf16, hi, preferred_element_type=F32)
            + jnp.dot(m_bf16, lo, preferred_element_type=F32))


def _rms(x):
    return x * lax.rsqrt(jnp.mean(x * x, axis=-1, keepdims=True) + EPS)


def _silu(x):
    return x * jax.nn.sigmoid(x)


def _softplus(x):
    return jnp.maximum(x, 0.0) + jnp.log(1.0 + jnp.exp(-jnp.abs(x)))


HIGH_HALF = 0xFFFF0000


def _slab_rows_per_token(d):
    return d // (2 * LANES)


def _to_slab(ref, x):
    n, d = x.shape
    ns = _slab_rows_per_token(d)
    bits = lambda v: pltpu.bitcast(v.astype(BF16).astype(F32), U32)
    for s in range(ns):
        lo = bits(x[:, s * LANES:(s + 1) * LANES])
        hi = bits(x[:, (s + ns) * LANES:(s + ns + 1) * LANES])
        ref[pl.ds(s, n, stride=ns), :] = (lo >> 16) | (hi & jnp.uint32(HIGH_HALF))


def _from_slab(ref, n):
    ns = ref.shape[0] // n
    words = [ref[pl.ds(s, n, stride=ns), :] for s in range(ns)]
    lo = [pltpu.bitcast(wd << 16, F32) for wd in words]
    hi = [pltpu.bitcast(wd & jnp.uint32(HIGH_HALF), F32) for wd in words]
    return jnp.concatenate(lo + hi, axis=1)


def _chunk_masks(tc):
    ri = lax.broadcasted_iota(I32, (tc, tc), 0)
    ci = lax.broadcasted_iota(I32, (tc, tc), 1)
    same = (ri // CHUNK) == (ci // CHUNK)
    causal = same & (ri >= ci)
    strict = same & (ri > ci)
    return ri, ci, causal, strict


def _proj_kernel(x_ref, nw_ref, w_ref, ws_ref, out_ref, small_ref, hn_ref):
    @pl.when(pl.program_id(1) == 0)
    def _():
        hn = (_rms(x_ref[...]) * nw_ref[...]).astype(BF16)
        hn_ref[...] = hn
        small_ref[...] = jnp.dot(hn, ws_ref[...], preferred_element_type=F32)

    out_ref[...] = jnp.dot(hn_ref[...], w_ref[...], preferred_element_type=F32).astype(BF16)


def _proj(x2, nw, w_main, w_small, tm, tn):
    t, d = x2.shape
    n = w_main.shape[1]
    return pl.pallas_call(
        _proj_kernel,
        grid=(t // tm, n // tn),
        in_specs=[
            pl.BlockSpec((tm, d), lambda i, j: (i, 0)),
            pl.BlockSpec((1, d), lambda i, j: (0, 0)),
            pl.BlockSpec((d, tn), lambda i, j: (0, j)),
            pl.BlockSpec((d, LANES), lambda i, j: (0, 0)),
        ],
        out_specs=[
            pl.BlockSpec((tm, tn), lambda i, j: (i, j)),
            pl.BlockSpec((tm, LANES), lambda i, j: (i, 0)),
        ],
        out_shape=[jax.ShapeDtypeStruct((t, n), BF16), jax.ShapeDtypeStruct((t, LANES), F32)],
        scratch_shapes=[pltpu.VMEM((tm, d), BF16)],
        compiler_params=_cparams(2),
        name="proj",
    )(x2, nw, w_main, w_small)


def _gdn_kernel(q_ref, k_ref, v_ref, sm_ref, cw_ref, alog_ref, dtb_ref, o_ref, xs_ref, st_ref, *, tc):
    w = HEADS * HEAD_DIM
    nc = tc // CHUNK

    @pl.when(pl.program_id(1) == 0)
    def _():
        xs_ref[0:8, :] = jnp.zeros((8, 3 * w), F32)
        st_ref[...] = jnp.zeros_like(st_ref)

    _, _, causal, strict = _chunk_masks(tc)
    tri = jnp.where(causal, 1.0, 0.0).astype(BF16)
    wi = lax.broadcasted_iota(I32, (CHUNK, tc), 0)
    wj = lax.broadcasted_iota(I32, (CHUNK, tc), 1)
    eye_w = jnp.where(wi == wj % CHUNK, 1.0, 0.0)
    blk_w = wj // CHUNK

    def fold(m_bd):
        acc = m_bd[0:CHUNK]
        for c in range(1, nc):
            acc = acc + m_bd[c * CHUNK:(c + 1) * CHUNK]
        return acc

    def unfold(m_w):
        return jnp.concatenate([jnp.where(blk_w == c, m_w, 0.0) for c in range(nc)], axis=0)

    sm = sm_ref[...]
    lane = lax.broadcasted_iota(I32, (tc, LANES), 1)
    g_all = jnp.where(lane < HEADS, -jnp.exp(alog_ref[...]) * _softplus(sm + dtb_ref[...]), 0.0)
    beta_all = jax.nn.sigmoid(sm)
    cum = _dot_exact_lhs(tri, g_all)
    ecum = jnp.exp(cum)
    cum_t = cum.T

    sr = lax.broadcasted_iota(I32, ((CONV_K - 1) * tc, tc), 0)
    sc = lax.broadcasted_iota(I32, ((CONV_K - 1) * tc, tc), 1)
    assert tc & (tc - 1) == 0
    shifts = jnp.where(sc == (sr & (tc - 1)) - ((sr >> (tc.bit_length() - 1)) + 1), 1.0, 0.0).astype(BF16)

    def conv_part(p, ref):
        cols = slice(p * w, (p + 1) * w)
        xb = ref[...]
        shifted = jnp.dot(shifts, xb, preferred_element_type=F32)
        acc = cw_ref[CONV_K - 1:CONV_K, cols] * xb.astype(F32)
        for s in range(1, CONV_K):
            acc = acc + cw_ref[CONV_K - 1 - s:CONV_K - s, cols] * shifted[(s - 1) * tc:s * tc]
        xs_ref[8:16, cols] = xb[0:8].astype(F32)
        first = cw_ref[CONV_K - 1:CONV_K, cols] * xs_ref[8:16, cols]
        for s in range(1, CONV_K):
            first = first + cw_ref[CONV_K - 1 - s:CONV_K - s, cols] * xs_ref[8 - s:16 - s, cols]
        xs_ref[0:8, cols] = xb[tc - 8:tc].astype(F32)
        return _silu(jnp.concatenate([first, acc[8:]], axis=0))

    qkv = [conv_part(p, ref) for p, ref in enumerate((q_ref, k_ref, v_ref))]

    def l2n(x):
        return x * lax.rsqrt(jnp.sum(x * x, axis=-1, keepdims=True) + EPS)

    heads = range(HEADS)
    ks, a_bd, qk_bd, rhs, qd, cc = [], [], [], [], [], []
    for h in heads:
        hc = slice(h * HEAD_DIM, (h + 1) * HEAD_DIM)
        q = l2n(qkv[0][:, hc]) * HEAD_DIM ** -0.5
        k = l2n(qkv[1][:, hc])
        v = qkv[2][:, hc]
        cch = cum[:, h:h + 1]
        cr = cum_t[h:h + 1, :]
        bc = beta_all[:, HEADS + h:HEADS + h + 1]
        ec = ecum[:, h:h + 1]
        dec = jnp.where(causal, jnp.exp(jnp.where(causal, cch - cr, 0.0)), 0.0)
        kb = k * bc
        a_bd.append(jnp.where(strict, _dot_nt(kb, k) * dec, 0.0))
        qk_bd.append(_dot_nt(q, k) * dec)
        rhs.append(jnp.concatenate([v * bc, kb * ec], axis=1).astype(BF16))
        qd.append(q * ec)
        ks.append(k)
        cc.append(cch)

    xw, pw = [], []
    for h in heads:
        aw = fold(a_bd[h])
        xw.append(eye_w - aw)
        pw.append(_dot(aw, a_bd[h]))
    n_sq = CHUNK.bit_length() - 2
    for it in range(n_sq):
        for h in heads:
            p_bd = unfold(pw[h]).astype(BF16)
            if it + 1 < n_sq:
                res = _dot(jnp.concatenate([xw[h], pw[h]], axis=0), p_bd)
                xw[h] = xw[h] + res[:CHUNK]
                pw[h] = res[CHUNK:]
            else:
                xw[h] = xw[h] + _dot(xw[h], p_bd)

    uw, qp, o0 = [], [], []
    for h in heads:
        uwh = _dot(unfold(xw[h]), rhs[h])
        qkuw = _dot(qk_bd[h], uwh)
        uw.append(uwh.astype(BF16))
        o0.append(qkuw[:, :HEAD_DIM])
        qp.append(qd[h] - qkuw[:, HEAD_DIM:])

    state = [st_ref[h] for h in heads]
    for c in range(nc):
        r0 = c * CHUNK
        rows = slice(r0, r0 + CHUNK)
        kuw, dlast = [], []
        for h in heads:
            last = cc[h][r0 + CHUNK - 1:r0 + CHUNK, :]
            kd = ks[h][rows] * jnp.exp(last - cc[h][rows])
            kuw.append(_dot_tn(kd, uw[h][rows]))
            dlast.append(jnp.exp(last))
        for h in heads:
            s = state[h]
            res = _dot(jnp.concatenate([kuw[h][:, HEAD_DIM:], qp[h][rows]], axis=0), s)
            o = res[HEAD_DIM:] + o0[h][rows]
            state[h] = s * dlast[h] - res[:HEAD_DIM] + kuw[h][:, :HEAD_DIM]
            o_ref[rows, h * HEAD_DIM:(h + 1) * HEAD_DIM] = _rms(o).astype(BF16)
    for h in heads:
        st_ref[h] = state[h]


def _gdn(p3, small3, conv_w, alog, dtb, tc):
    b, s, _ = p3.shape
    w = HEADS * HEAD_DIM
    return pl.pallas_call(
        functools.partial(_gdn_kernel, tc=tc),
        grid=(b, s // tc),
        in_specs=[
            pl.BlockSpec((None, tc, w), lambda i, j: (i, j, 0)),
            pl.BlockSpec((None, tc, w), lambda i, j: (i, j, 1)),
            pl.BlockSpec((None, tc, w), lambda i, j: (i, j, 2)),
            pl.BlockSpec((None, tc, LANES), lambda i, j: (i, j, 0)),
            pl.BlockSpec((CONV_K, 3 * w), lambda i, j: (0, 0)),
            pl.BlockSpec((1, LANES), lambda i, j: (0, 0)),
            pl.BlockSpec((1, LANES), lambda i, j: (0, 0)),
        ],
        out_specs=pl.BlockSpec((None, tc, w), lambda i, j: (i, j, 0)),
        out_shape=jax.ShapeDtypeStruct((b, s, w), BF16),
        scratch_shapes=[pltpu.VMEM((16, 3 * w), F32), pltpu.VMEM((HEADS, HEAD_DIM, HEAD_DIM), F32)],
        compiler_params=_cparams(2),
        name="gdn",
    )(p3, p3, p3, small3, conv_w, alog, dtb)


def _hgrn_kernel(f_ref, q_ref, i_ref, lb_ref, o_ref, st_ref, *, tc):
    nc = tc // CHUNK

    @pl.when(pl.program_id(1) == 0)
    def _():
        st_ref[...] = jnp.zeros_like(st_ref)

    _, _, causal, _ = _chunk_masks(tc)
    tri = jnp.where(causal, 1.0, 0.0).astype(BF16)

    lbp = lb_ref[...]
    lbe = jnp.exp(lbp - jnp.max(lbp, axis=0, keepdims=True))
    lb = lbe[0:1, :] / jnp.sum(lbe, axis=0, keepdims=True)

    heads = range(HEADS)
    hcols = [slice(h * HEAD_DIM, (h + 1) * HEAD_DIM) for h in heads]
    forget = lb + (1.0 - lb) * jax.nn.sigmoid(f_ref[...].astype(F32))
    cum = _dot_exact_lhs(tri, jnp.log(forget))
    kk = 1.0 - forget
    q_in = (q_ref[...].astype(F32) * HEAD_DIM ** -0.5 * jnp.exp(cum)).astype(BF16)
    k_in = (kk * jnp.exp(-cum)).astype(BF16)
    v = i_ref[...]
    intra = [jnp.where(causal, _dot_nt(q_in[:, hc], k_in[:, hc]), 0.0).astype(BF16) for hc in hcols]
    state = [st_ref[h] for h in heads]
    for c in range(nc):
        r0 = c * CHUNK
        rows = slice(r0, r0 + CHUNK)
        last = cum[r0 + CHUNK - 1:r0 + CHUNK, :]
        k_dec = (kk[rows] * jnp.exp(last - cum[rows])).astype(BF16)
        dlast = jnp.exp(last)
        for h, hc in zip(heads, hcols):
            o = _dot_nt(q_in[rows, hc], state[h]) + _dot(intra[h][rows, rows], v[rows, hc])
            state[h] = state[h] * dlast[:, hc] + _dot_tn(v[rows, hc], k_dec[:, hc])
            o_ref[rows, hc] = _rms(o).astype(BF16)
    for h in heads:
        st_ref[h] = state[h]


def _hgrn(p3, lb_logits, tc):
    b, s, _ = p3.shape
    w = HEADS * HEAD_DIM
    return pl.pallas_call(
        functools.partial(_hgrn_kernel, tc=tc),
        grid=(b, s // tc),
        in_specs=[
            pl.BlockSpec((None, tc, w), lambda i, j: (i, j, 4)),
            pl.BlockSpec((None, tc, w), lambda i, j: (i, j, 5)),
            pl.BlockSpec((None, tc, w), lambda i, j: (i, j, 6)),
            pl.BlockSpec(lb_logits.shape, lambda i, j: (0, 0)),
        ],
        out_specs=pl.BlockSpec((None, tc, w), lambda i, j: (i, j, 0)),
        out_shape=jax.ShapeDtypeStruct((b, s, w), BF16),
        scratch_shapes=[pltpu.VMEM((HEADS, HEAD_DIM, HEAD_DIM), F32)],
        compiler_params=_cparams(2),
        name="hgrn",
    )(p3, p3, p3, lb_logits)


def _kv_kernel(mem_ref, nw_ref, wkv_ref, kv_ref):
    mn = (_rms(mem_ref[...]) * nw_ref[...]).astype(BF16)
    kv_ref[...] = jnp.dot(mn, wkv_ref[...], preferred_element_type=F32).astype(BF16)


def _kv(mem, nw, wkv):
    b, m, d = mem.shape
    return pl.pallas_call(
        _kv_kernel,
        grid=(b,),
        in_specs=[
            pl.BlockSpec((None, m, d), lambda i: (i, 0, 0)),
            pl.BlockSpec((1, d), lambda i: (0, 0)),
            pl.BlockSpec(wkv.shape, lambda i: (0, 0)),
        ],
        out_specs=pl.BlockSpec((None, m, 2 * d), lambda i: (i, 0, 0)),
        out_shape=jax.ShapeDtypeStruct((b, m, 2 * d), BF16),
        compiler_params=_cparams(1),
        name="kv",
    )(mem, nw, wkv)


def _post_kernel(x_ref, oa_ref, ob_ref, oga_ref, ogb_ref, ga_ref, gb_ref, kv_ref,
                 gnw_ref, hnw_ref, wa_ref, wb_ref, wout_ref, nx_ref, wq_ref, wo_ref, nf_ref,
                 wr_ref, br_ref, h2_ref, hn3_ref, lg_ref):
    d = x_ref.shape[-1]
    dh = d // XA_HEADS
    ya = oa_ref[...].astype(F32) * gnw_ref[...] * _silu(oga_ref[...].astype(F32))
    yb = ob_ref[...].astype(F32) * hnw_ref[...] * _silu(ogb_ref[...].astype(F32))
    merged = (jax.nn.sigmoid(ga_ref[...].astype(F32)) * _dot(ya, wa_ref[...])
              + jax.nn.sigmoid(gb_ref[...].astype(F32)) * _dot(yb, wb_ref[...]))
    h1 = x_ref[...] + _dot(merged, wout_ref[...])

    q = _dot(_rms(h1) * nx_ref[...], wq_ref[...]) * dh ** -0.5
    outs = []
    for hh in range(XA_HEADS):
        kh = kv_ref[:, hh * dh:(hh + 1) * dh]
        vh = kv_ref[:, d + hh * dh:d + (hh + 1) * dh]
        sc = _dot_nt(q[:, hh * dh:(hh + 1) * dh], kh)
        p = jnp.exp(sc - jnp.max(sc, axis=-1, keepdims=True))
        outs.append(_dot(p, vh) / jnp.sum(p, axis=-1, keepdims=True))
    h2 = h1 + _dot(jnp.concatenate(outs, axis=1), wo_ref[...])
    h2_ref[...] = h2

    hn3 = _rms(h2) * nf_ref[...]
    _to_slab(hn3_ref, hn3)
    hi, lo = _split(hn3)
    whi, wlo = _split(wr_ref[...])
    lg_ref[...] = (jnp.dot(hi, whi, preferred_element_type=F32)
                   + jnp.dot(hi, wlo, preferred_element_type=F32)
                   + jnp.dot(lo, whi, preferred_element_type=F32)) + br_ref[...]


def _post(x3, oa, ob, p3, kv, gnw, hnw, wa, wb, wout, nx, wq, wo, nf, wr, br, tm):
    b, s, d = x3.shape
    ns = _slab_rows_per_token(d)
    row = lambda c: pl.BlockSpec((None, tm, d), lambda i, j: (i, j, c))
    full = lambda a: pl.BlockSpec(a.shape, lambda i, j: (0,) * a.ndim, pipeline_mode=pl.Buffered(1))
    return pl.pallas_call(
        _post_kernel,
        grid=(b, s // tm),
        in_specs=[row(0), row(0), row(0), row(3), row(7), row(8), row(9),
                  pl.BlockSpec((None,) + kv.shape[1:], lambda i, j: (i, 0, 0)),
                  full(gnw), full(hnw), full(wa), full(wb), full(wout), full(nx), full(wq), full(wo),
                  full(nf), full(wr), full(br)],
        out_specs=[row(0), pl.BlockSpec((tm * ns, LANES), lambda i, j: (i * (s // tm) + j, 0)),
                   pl.BlockSpec((None, tm, LANES), lambda i, j: (i, j, 0))],
        out_shape=[jax.ShapeDtypeStruct((b, s, d), F32), jax.ShapeDtypeStruct((b * s * ns, LANES), U32),
                   jax.ShapeDtypeStruct((b, s, LANES), F32)],
        compiler_params=_cparams(2),
        name="post",
    )(x3, oa, ob, p3, p3, p3, p3, kv, gnw, hnw, wa, wb, wout, nx, wq, wo, nf, wr, br)


def _route_kernel(lg_ref, code_ref, wt_ref, cnt_ref, carry_ref, *, tr):
    @pl.when(pl.program_id(0) == 0)
    def _():
        carry_ref[...] = jnp.zeros_like(carry_ref)

    neg = -1e30
    big = 2 * LANES
    lg = lg_ref[...]
    lane = lax.broadcasted_iota(I32, (tr, LANES), 1)
    lane_f = lane.astype(F32)
    first = lambda m: jnp.min(jnp.where(m, lane_f, big), axis=-1, keepdims=True).astype(I32)

    is_g = lane < N_GROUPS
    gl = jnp.where(is_g, lg, neg)
    gmax = jnp.max(gl, axis=-1, keepdims=True)
    gidx = first(gl == gmax)
    g_p = 1.0 / jnp.sum(jnp.where(is_g, jnp.exp(gl - gmax), 0.0), axis=-1, keepdims=True)

    lo_lane = N_GROUPS + gidx * EXP_PER_GROUP
    in_grp = (lane >= lo_lane) & (lane < lo_lane + EXP_PER_GROUP)
    el = jnp.where(in_grp, lg, neg)
    m1 = jnp.max(el, axis=-1, keepdims=True)
    i1 = first(el == m1)
    el2 = jnp.where(lane == i1, neg, el)
    m2 = jnp.max(el2, axis=-1, keepdims=True)
    i2 = first(el2 == m2)
    esum = jnp.sum(jnp.where(in_grp, jnp.exp(el - m1), 0.0), axis=-1, keepdims=True)
    p1 = 1.0 / esum
    p2 = jnp.exp(m2 - m1) / esum
    w1 = g_p * p1 / (p1 + p2)
    w2 = g_p * p2 / (p1 + p2)
    e1 = i1 - N_GROUPS
    e2 = i2 - N_GROUPS

    oh1 = lane == e1
    oh2 = lane == e2
    both = jnp.where(oh1 | oh2, 1.0, 0.0)
    ri = lax.broadcasted_iota(I32, (tr, tr), 0)
    ci = lax.broadcasted_iota(I32, (tr, tr), 1)
    below = jnp.where(ri > ci, 1.0, 0.0).astype(BF16)
    before = jnp.dot(below, both.astype(BF16), preferred_element_type=F32) + carry_ref[...]
    r1 = jnp.sum(jnp.where(oh1, before, 0.0), axis=-1, keepdims=True).astype(I32)
    r2 = jnp.sum(jnp.where(oh2, before, 0.0), axis=-1, keepdims=True).astype(I32)
    carry_ref[...] = carry_ref[...] + jnp.sum(both, axis=0, keepdims=True)

    code_ref[...] = jnp.where(lane == 0, e1 * (1 << RANK_BITS) + r1,
                              jnp.where(lane == 1, e2 * (1 << RANK_BITS) + r2, 0))
    wt_ref[...] = jnp.where(lane == 0, w1, jnp.where(lane == 1, w2, 0.0))
    cnt_ref[...] = carry_ref[...]


def _route(logits, tr):
    t = logits.shape[0]
    return pl.pallas_call(
        functools.partial(_route_kernel, tr=tr),
        grid=(t // tr,),
        in_specs=[pl.BlockSpec((tr, LANES), lambda i: (i, 0))],
        out_specs=[pl.BlockSpec((tr, LANES), lambda i: (i, 0)),
                   pl.BlockSpec((tr, LANES), lambda i: (i, 0)),
                   pl.BlockSpec((1, LANES), lambda i: (0, 0))],
        out_shape=[jax.ShapeDtypeStruct((t, LANES), I32), jax.ShapeDtypeStruct((t, LANES), F32),
                   jax.ShapeDtypeStruct((1, LANES), F32)],
        scratch_shapes=[pltpu.VMEM((1, LANES), F32)],
        compiler_params=_cparams(1),
        name="route",
    )(logits)


def _dest_kernel(code_ref, ps_ref, out_ref):
    code = code_ref[...]
    lane = lax.broadcasted_iota(I32, code.shape, 1)
    ps = ps_ref[...]

    def dest(slot):
        c = code[:, slot:slot + 1]
        start = jnp.sum(jnp.where(lane == (c >> RANK_BITS), ps, 0.0), axis=-1, keepdims=True)
        return (c & ((1 << RANK_BITS) - 1)).astype(F32) + start

    both = jnp.where(lane == 0, dest(0), jnp.where(lane == 1, dest(1), 0.0))
    out_ref[...] = both.T[0:8, :].astype(I32)


def _dest(codes, pstart_lanes, tr):
    t = codes.shape[0]
    out = pl.pallas_call(
        _dest_kernel,
        grid=(t // tr,),
        in_specs=[pl.BlockSpec((tr, LANES), lambda i: (i, 0)),
                  pl.BlockSpec((1, LANES), lambda i: (0, 0))],
        out_specs=pl.BlockSpec((None, 8, tr), lambda i: (i, 0, 0)),
        out_shape=jax.ShapeDtypeStruct((t // tr, 8, tr), I32),
        compiler_params=_cparams(1),
        name="dest",
    )(codes, pstart_lanes)
    return out[:, 0, :].reshape(t), out[:, 1, :].reshape(t)


ISSUE_UNROLL = 8


def _slab_rows(ref, row, ns):
    return ref.at[pl.ds(pl.multiple_of(row * ns, ns), ns)]


def _dispatch_kernel(ps_ref, sz_ref, d1_ref, d2_ref, hn_ref, xp_ref, zero_ref, sem, *, tr, ns):
    step = pl.program_id(0)
    base = step * tr

    def copy(i, d):
        return pltpu.make_async_copy(_slab_rows(hn_ref, i, ns), _slab_rows(xp_ref, d, ns), sem)

    def pad_copy(d):
        return pltpu.make_async_copy(zero_ref.at[pl.ds(0, ns)], _slab_rows(xp_ref, d, ns), sem)

    def pad_block_copy(blk):
        return pltpu.make_async_copy(zero_ref, _slab_rows(xp_ref, blk, MOE_BLOCK * ns), sem)

    def issue(g, carry):
        for u in range(ISSUE_UNROLL):
            i = g * ISSUE_UNROLL + u
            copy(i, d1_ref[base + i]).start(priority=0)
            copy(i, d2_ref[base + i]).start(priority=1)
        return carry

    def drain(g, carry):
        for _ in range(2 * ISSUE_UNROLL):
            copy(0, 0).wait()
        return carry

    lax.fori_loop(0, tr // ISSUE_UNROLL, issue, 0)
    lax.fori_loop(0, tr // ISSUE_UNROLL, drain, 0)

    @pl.when(step == pl.num_programs(0) - 1)
    def _():
        zero_ref[...] = jnp.zeros_like(zero_ref)

        def pad_expert(e, n_pad):
            n = sz_ref[e]
            first = ps_ref[e] + n
            n_e = (MOE_BLOCK - n % MOE_BLOCK) % MOE_BLOCK

            def one(r, c):
                pad_copy(first + r).start()
                return c

            lax.fori_loop(0, n_e, one, 0)
            return n_pad + n_e

        n_pad = lax.fori_loop(0, N_EXPERTS, pad_expert, 0)

        def drain_pad(r, c):
            pad_copy(0).wait()
            return c

        lax.fori_loop(0, n_pad, drain_pad, 0)

        first_blk = (ps_ref[N_EXPERTS - 1] + sz_ref[N_EXPERTS - 1] + MOE_BLOCK - 1) // MOE_BLOCK
        n_blocks = xp_ref.shape[0] // (MOE_BLOCK * ns)

        def tail_start(blk, c):
            pad_block_copy(blk).start()
            return c

        def tail_wait(blk, c):
            pad_block_copy(blk).wait()
            return c

        lax.fori_loop(first_blk, n_blocks, tail_start, 0)
        lax.fori_loop(first_blk, n_blocks, tail_wait, 0)


def _dispatch(pstart, sizes, dest1, dest2, hn3_slab, n_rows, tr, ns):
    t = dest1.shape[0]
    return pl.pallas_call(
        functools.partial(_dispatch_kernel, tr=tr, ns=ns),
        grid_spec=pltpu.PrefetchScalarGridSpec(
            num_scalar_prefetch=4,
            grid=(t // tr,),
            in_specs=[pl.BlockSpec((tr * ns, LANES), lambda i, *_: (i, 0))],
            out_specs=pl.BlockSpec(memory_space=pl.ANY),
            scratch_shapes=[pltpu.VMEM((MOE_BLOCK * ns, LANES), U32), pltpu.SemaphoreType.DMA],
        ),
        out_shape=jax.ShapeDtypeStruct((n_rows * ns, LANES), U32),
        compiler_params=_cparams(1),
        name="dispatch",
    )(pstart, sizes, dest1, dest2, hn3_slab)


def _expert_kernel(be_ref, nu_ref, x_ref, wg_ref, wu_ref, wd_ref, y_ref):
    del be_ref

    @pl.when(pl.program_id(0) < nu_ref[0])
    def _():
        x = _from_slab(x_ref, MOE_BLOCK).astype(BF16)
        hmid = _silu(_dot(x, wg_ref[...])) * _dot(x, wu_ref[...])
        _to_slab(y_ref, _dot(hmid, wd_ref[...]))

    @pl.when(pl.program_id(0) >= nu_ref[0])
    def _():
        y_ref[...] = jnp.zeros_like(y_ref)


def _experts(blk_e, n_used, x_pad, wg, wu, wd):
    d, ff = wg.shape[-2:]
    blk = MOE_BLOCK * _slab_rows_per_token(d)
    n_blocks = x_pad.shape[0] // blk
    rows = lambda i, be, nu: (jnp.minimum(i, nu[0] - 1), 0)
    return pl.pallas_call(
        _expert_kernel,
        grid_spec=pltpu.PrefetchScalarGridSpec(
            num_scalar_prefetch=2,
            grid=(n_blocks,),
            in_specs=[pl.BlockSpec((blk, LANES), rows),
                      pl.BlockSpec((None, d, ff), lambda i, be, nu: (be[i], 0, 0)),
                      pl.BlockSpec((None, d, ff), lambda i, be, nu: (be[i], 0, 0)),
                      pl.BlockSpec((None, ff, d), lambda i, be, nu: (be[i], 0, 0))],
            out_specs=pl.BlockSpec((blk, LANES), lambda i, be, nu: (i, 0)),
        ),
        out_shape=jax.ShapeDtypeStruct(x_pad.shape, U32),
        compiler_params=_cparams(1),
        name="experts",
    )(blk_e, n_used, x_pad, wg, wu, wd)


def _combine_kernel(d1_ref, d2_ref, h2_ref, wt_ref, fnw_ref, y_ref, out_ref, b1_ref, b2_ref, sem, *, tr, ns):
    step = pl.program_id(0)
    n_steps = pl.num_programs(0)

    def copy(d, buf, slot, i):
        return pltpu.make_async_copy(_slab_rows(y_ref, d, ns), _slab_rows(buf.at[slot], i, ns), sem.at[slot])

    def issue_step(st):
        slot = st % 2

        def issue(g, carry):
            for u in range(ISSUE_UNROLL):
                i = g * ISSUE_UNROLL + u
                copy(d1_ref[st * tr + i], b1_ref, slot, i).start(priority=0)
                copy(d2_ref[st * tr + i], b2_ref, slot, i).start(priority=1)
            return carry

        lax.fori_loop(0, tr // ISSUE_UNROLL, issue, 0)

    @pl.when(step == 0)
    def _():
        issue_step(step)

    @pl.when(step + 1 < n_steps)
    def _():
        issue_step(step + 1)

    slot = step % 2

    def drain(g, carry):
        for _ in range(ISSUE_UNROLL):
            copy(0, b1_ref, slot, 0).wait()
            copy(0, b2_ref, slot, 0).wait()
        return carry

    lax.fori_loop(0, tr // ISSUE_UNROLL, drain, 0)

    wt = wt_ref[...]
    h3 = (h2_ref[...] + wt[:, 0:1] * _from_slab(b1_ref.at[slot], tr)
          + wt[:, 1:2] * _from_slab(b2_ref.at[slot], tr))
    out_ref[...] = _rms(h3) * fnw_ref[...]


def _combine(dest1, dest2, h2, wts, fnw, y_pad, tr):
    t, d = h2.shape
    ns = _slab_rows_per_token(d)
    return pl.pallas_call(
        functools.partial(_combine_kernel, tr=tr, ns=ns),
        grid_spec=pltpu.PrefetchScalarGridSpec(
            num_scalar_prefetch=2,
            grid=(t // tr,),
            in_specs=[pl.BlockSpec((tr, d), lambda i, *_: (i, 0)),
                      pl.BlockSpec((tr, LANES), lambda i, *_: (i, 0)),
                      pl.BlockSpec((1, d), lambda i, *_: (0, 0)),
                      pl.BlockSpec(memory_space=pl.ANY)],
            out_specs=pl.BlockSpec((tr, d), lambda i, *_: (i, 0)),
            scratch_shapes=[pltpu.VMEM((2, tr * ns, LANES), U32), pltpu.VMEM((2, tr * ns, LANES), U32),
                            pltpu.SemaphoreType.DMA((2,))],
        ),
        out_shape=jax.ShapeDtypeStruct((t, d), F32),
        compiler_params=_cparams(1),
        name="combine",
    )(dest1, dest2, h2, wts, fnw, y_pad)


def _pick(n, pref):
    while n % pref:
        pref //= 2
    return pref


def kernel(x, mem, norm_mix_w, w_in, conv_w, gdn_a_log, gdn_dt_bias, gdn_out_norm_w, hgrn_lb, hgrn_out_norm_w, w_branch_a, w_branch_b, w_out, norm_xattn_w, norm_mem_w, xattn_wq, xattn_wkv, xattn_wo, norm_ffn_w, router_group_w, router_group_b, router_expert_w, router_expert_b, expert_w_gate, expert_w_up, expert_w_down, final_norm_w):
    b, s, d = x.shape
    t = b * s
    depth = w_in.shape[0]
    w = HEADS * HEAD_DIM
    qkv_w = 3 * w
    assert d == w and s % CHUNK == 0

    tc = _pick(s, 256)
    tm_proj = _pick(t, 1024)
    tm_post = _pick(s, 512)
    tr = _pick(t, 256)

    def pad_lanes(v, offset=0):
        return jnp.zeros((1, LANES), F32).at[0, offset:offset + v.shape[0]].set(v.astype(F32))

    assert depth == 1
    h3d = x
    for layer in range(depth):
        wl = w_in[layer]
        w_main = jnp.concatenate([wl[:, :qkv_w], wl[:, qkv_w + 2 * HEADS:]], axis=1).astype(BF16)
        w_small = jnp.zeros((d, LANES), F32).at[:, :2 * HEADS].set(wl[:, qkv_w:qkv_w + 2 * HEADS]).astype(BF16)
        p_main, p_small = _proj(h3d.reshape(t, d), norm_mix_w[layer][None, :], w_main, w_small,
                                tm_proj, 2048)
        p3 = p_main.reshape(b, s, -1)
        o_a = _gdn(p3, p_small.reshape(b, s, LANES), conv_w[layer], pad_lanes(gdn_a_log[layer]),
                   pad_lanes(gdn_dt_bias[layer]), tc)
        o_b = _hgrn(p3, hgrn_lb[layer:], tc)
        kv = _kv(mem, norm_mem_w[layer][None, :], xattn_wkv[layer].astype(BF16))
        w_router = jnp.zeros((d, LANES), F32)
        w_router = w_router.at[:, :N_GROUPS].set(router_group_w[layer])
        w_router = w_router.at[:, N_GROUPS:N_GROUPS + N_EXPERTS].set(router_expert_w[layer])
        b_router = pad_lanes(router_group_b[layer]) + pad_lanes(router_expert_b[layer], N_GROUPS)
        tile8 = lambda v: jnp.tile(v.astype(F32), HEADS)[None, :]
        h2, hn3, logits = _post(
            h3d, o_a, o_b, p3, kv, tile8(gdn_out_norm_w[layer]), tile8(hgrn_out_norm_w[layer]),
            w_branch_a[layer].astype(BF16), w_branch_b[layer].astype(BF16), w_out[layer].astype(BF16),
            norm_xattn_w[layer][None, :], xattn_wq[layer].astype(BF16), xattn_wo[layer].astype(BF16),
            norm_ffn_w[layer][None, :], w_router, b_router, tm_post)

        codes, wts, counts = _route(logits.reshape(t, LANES), tr)
        sizes = counts[0, :N_EXPERTS].astype(I32)
        padded = ((sizes + MOE_BLOCK - 1) // MOE_BLOCK) * MOE_BLOCK
        pend = jnp.cumsum(padded)
        pstart = (pend - padded).astype(I32)
        m = t * 2
        n_rows = ((m + MOE_BLOCK - 1) // MOE_BLOCK) * MOE_BLOCK + N_EXPERTS * MOE_BLOCK
        n_blocks = n_rows // MOE_BLOCK
        blk_start = jnp.arange(n_blocks, dtype=I32) * MOE_BLOCK
        blk_e = jnp.minimum(jnp.sum(pend[None, :] <= blk_start[:, None], axis=1), N_EXPERTS - 1).astype(I32)
        n_used = (pend[-1:] // MOE_BLOCK).astype(I32)
        dest1, dest2 = _dest(codes, pad_lanes(pstart), tr)

        x_pad = _dispatch(pstart, sizes, dest1, dest2, hn3, n_rows, _pick(t, 1024), _slab_rows_per_token(d))
        y_pad = _experts(blk_e, n_used, x_pad, expert_w_gate[layer], expert_w_up[layer], expert_w_down[layer])
        out = _combine(dest1, dest2, h2.reshape(t, d), wts, final_norm_w[None, :], y_pad, tr)
        h3d = out.reshape(b, s, d)
    return h3d
```

```python
import functools

import jax
import jax.numpy as jnp
from jax import lax
from jax.experimental import pallas as pl
from jax.experimental.pallas import tpu as pltpu

F32 = jnp.float32
BF16 = jnp.bfloat16
I32 = jnp.int32
U32 = jnp.uint32

EPS = 1e-6
CHUNK = 64
HEADS = 8
HEAD_DIM = 128
CONV_K = 4
XA_HEADS = 4
N_GROUPS = 4
EXP_PER_GROUP = 8
N_EXPERTS = N_GROUPS * EXP_PER_GROUP
MOE_BLOCK = 256
LANES = 128
RANK_BITS = 20

VMEM_LIMIT = 52 * 1024 * 1024


def _cparams(n_axes):
    return pltpu.CompilerParams(dimension_semantics=("arbitrary",) * n_axes,
                                vmem_limit_bytes=VMEM_LIMIT)


def _dot(a, b):
    return jnp.dot(a.astype(BF16), b.astype(BF16), preferred_element_type=F32)


def _dot_nt(a, b):
    return lax.dot_general(a.astype(BF16), b.astype(BF16), (((1,), (1,)), ((), ())),
                           preferred_element_type=F32)


def _dot_tn(a, b):
    return lax.dot_general(a.astype(BF16), b.astype(BF16), (((0,), (0,)), ((), ())),
                           preferred_element_type=F32)


def _split(x):
    hi = x.astype(BF16)
    lo = (x - hi.astype(F32)).astype(BF16)
    return hi, lo


def _dot_exact_lhs(m_bf16, x):
    hi, lo = _split(x)
    return (jnp.dot(m_bf16, hi, preferred_element_type=F32)
            + jnp.dot(m_bf16, lo, preferred_element_type=F32))


def _rms(x):
    return x * lax.rsqrt(jnp.mean(x * x, axis=-1, keepdims=True) + EPS)


def _silu(x):
    return x * jax.nn.sigmoid(x)


def _softplus(x):
    return jnp.maximum(x, 0.0) + jnp.log(1.0 + jnp.exp(-jnp.abs(x)))


HIGH_HALF = 0xFFFF0000


def _slab_rows_per_token(d):
    return d // (2 * LANES)


def _to_slab(ref, x):
    n, d = x.shape
    ns = _slab_rows_per_token(d)
    bits = lambda v: pltpu.bitcast(v.astype(BF16).astype(F32), U32)
    for s in range(ns):
        lo = bits(x[:, s * LANES:(s + 1) * LANES])
        hi = bits(x[:, (s + ns) * LANES:(s + ns + 1) * LANES])
        ref[pl.ds(s, n, stride=ns), :] = (lo >> 16) | (hi & jnp.uint32(HIGH_HALF))


def _from_slab(ref, n):
    ns = ref.shape[0] // n
    words = [ref[pl.ds(s, n, stride=ns), :] for s in range(ns)]
    lo = [pltpu.bitcast(wd << 16, F32) for wd in words]
    hi = [pltpu.bitcast(wd & jnp.uint32(HIGH_HALF), F32) for wd in words]
    return jnp.concatenate(lo + hi, axis=1)


def _chunk_masks(tc):
    ri = lax.broadcasted_iota(I32, (tc, tc), 0)
    ci = lax.broadcasted_iota(I32, (tc, tc), 1)
    same = (ri // CHUNK) == (ci // CHUNK)
    causal = same & (ri >= ci)
    strict = same & (ri > ci)
    return ri, ci, causal, strict


def _proj_kernel(x_ref, nw_ref, w_ref, ws_ref, out_ref, small_ref, hn_ref):
    @pl.when(pl.program_id(1) == 0)
    def _():
        hn = (_rms(x_ref[...]) * nw_ref[...]).astype(BF16)
        hn_ref[...] = hn
        small_ref[...] = jnp.dot(hn, ws_ref[...], preferred_element_type=F32)

    out_ref[...] = jnp.dot(hn_ref[...], w_ref[...], preferred_element_type=F32).astype(BF16)


def _proj(x2, nw, w_main, w_small, tm, tn):
    t, d = x2.shape
    n = w_main.shape[1]
    return pl.pallas_call(
        _proj_kernel,
        grid=(t // tm, n // tn),
        in_specs=[
            pl.BlockSpec((tm, d), lambda i, j: (i, 0)),
            pl.BlockSpec((1, d), lambda i, j: (0, 0)),
            pl.BlockSpec((d, tn), lambda i, j: (0, j)),
            pl.BlockSpec((d, LANES), lambda i, j: (0, 0)),
        ],
        out_specs=[
            pl.BlockSpec((tm, tn), lambda i, j: (i, j)),
            pl.BlockSpec((tm, LANES), lambda i, j: (i, 0)),
        ],
        out_shape=[jax.ShapeDtypeStruct((t, n), BF16), jax.ShapeDtypeStruct((t, LANES), F32)],
        scratch_shapes=[pltpu.VMEM((tm, d), BF16)],
        compiler_params=_cparams(2),
        name="proj",
    )(x2, nw, w_main, w_small)


def _gdn_kernel(q_ref, k_ref, v_ref, sm_ref, cw_ref, alog_ref, dtb_ref, o_ref, xs_ref, st_ref, *, tc):
    w = HEADS * HEAD_DIM
    nc = tc // CHUNK

    @pl.when(pl.program_id(1) == 0)
    def _():
        xs_ref[0:8, :] = jnp.zeros((8, 3 * w), F32)
        st_ref[...] = jnp.zeros_like(st_ref)

    _, _, causal, strict = _chunk_masks(tc)
    tri = jnp.where(causal, 1.0, 0.0).astype(BF16)
    wi = lax.broadcasted_iota(I32, (CHUNK, tc), 0)
    wj = lax.broadcasted_iota(I32, (CHUNK, tc), 1)
    eye_w = jnp.where(wi == wj % CHUNK, 1.0, 0.0)
    blk_w = wj // CHUNK

    def fold(m_bd):
        acc = m_bd[0:CHUNK]
        for c in range(1, nc):
            acc = acc + m_bd[c * CHUNK:(c + 1) * CHUNK]
        return acc

    def unfold(m_w):
        return jnp.concatenate([jnp.where(blk_w == c, m_w, 0.0) for c in range(nc)], axis=0)

    sm = sm_ref[...]
    lane = lax.broadcasted_iota(I32, (tc, LANES), 1)
    g_all = jnp.where(lane < HEADS, -jnp.exp(alog_ref[...]) * _softplus(sm + dtb_ref[...]), 0.0)
    beta_all = jax.nn.sigmoid(sm)
    cum = _dot_exact_lhs(tri, g_all)
    ecum = jnp.exp(cum)
    cum_t = cum.T

    sr = lax.broadcasted_iota(I32, ((CONV_K - 1) * tc, tc), 0)
    sc = lax.broadcasted_iota(I32, ((CONV_K - 1) * tc, tc), 1)
    assert tc & (tc - 1) == 0
    shifts = jnp.where(sc == (sr & (tc - 1)) - ((sr >> (tc.bit_length() - 1)) + 1), 1.0, 0.0).astype(BF16)

    def conv_part(p, ref):
        cols = slice(p * w, (p + 1) * w)
        xb = ref[...]
        shifted = jnp.dot(shifts, xb, preferred_element_type=F32)
        acc = cw_ref[CONV_K - 1:CONV_K, cols] * xb.astype(F32)
        for s in range(1, CONV_K):
            acc = acc + cw_ref[CONV_K - 1 - s:CONV_K - s, cols] * shifted[(s - 1) * tc:s * tc]
        xs_ref[8:16, cols] = xb[0:8].astype(F32)
        first = cw_ref[CONV_K - 1:CONV_K, cols] * xs_ref[8:16, cols]
        for s in range(1, CONV_K):
            first = first + cw_ref[CONV_K - 1 - s:CONV_K - s, cols] * xs_ref[8 - s:16 - s, cols]
        xs_ref[0:8, cols] = xb[tc - 8:tc].astype(F32)
        return _silu(jnp.concatenate([first, acc[8:]], axis=0))

    qkv = [conv_part(p, ref) for p, ref in enumerate((q_ref, k_ref, v_ref))]

    def l2n(x):
        return x * lax.rsqrt(jnp.sum(x * x, axis=-1, keepdims=True) + EPS)

    heads = range(HEADS)
    ks, a_bd, qk_bd, rhs, qd, cc = [], [], [], [], [], []
    for h in heads:
        hc = slice(h * HEAD_DIM, (h + 1) * HEAD_DIM)
        q = l2n(qkv[0][:, hc]) * HEAD_DIM ** -0.5
        k = l2n(qkv[1][:, hc])
        v = qkv[2][:, hc]
        cch = cum[:, h:h + 1]
        cr = cum_t[h:h + 1, :]
        bc = beta_all[:, HEADS + h:HEADS + h + 1]
        ec = ecum[:, h:h + 1]
        dec = jnp.where(causal, jnp.exp(jnp.where(causal, cch - cr, 0.0)), 0.0)
        kb = k * bc
        a_bd.append(jnp.where(strict, _dot_nt(kb, k) * dec, 0.0))
        qk_bd.append(_dot_nt(q, k) * dec)
        rhs.append(jnp.concatenate([v * bc, kb * ec], axis=1).astype(BF16))
        qd.append(q * ec)
        ks.append(k)
        cc.append(cch)

    xw, pw = [], []
    for h in heads:
        aw = fold(a_bd[h])
        xw.append(eye_w - aw)
        pw.append(_dot(aw, a_bd[h]))
    n_sq = CHUNK.bit_length() - 2
    for it in range(n_sq):
        for h in heads:
            p_bd = unfold(pw[h]).astype(BF16)
            if it + 1 < n_sq:
                res = _dot(jnp.concatenate([xw[h], pw[h]], axis=0), p_bd)
                xw[h] = xw[h] + res[:CHUNK]
                pw[h] = res[CHUNK:]
            else:
                xw[h] = xw[h] + _dot(xw[h], p_bd)

    uw, qp, o0 = [], [], []
    for h in heads:
        uwh = _dot(unfold(xw[h]), rhs[h])
        qkuw = _dot(qk_bd[h], uwh)
        uw.append(uwh.astype(BF16))
        o0.append(qkuw[:, :HEAD_DIM])
        qp.append(qd[h] - qkuw[:, HEAD_DIM:])

    state = [st_ref[h] for h in heads]
    for c in range(nc):
        r0 = c * CHUNK
        rows = slice(r0, r0 + CHUNK)
        kuw, dlast = [], []
        for h in heads:
            last = cc[h][r0 + CHUNK - 1:r0 + CHUNK, :]
            kd = ks[h][rows] * jnp.exp(last - cc[h][rows])
            kuw.append(_dot_tn(kd, uw[h][rows]))
            dlast.append(jnp.exp(last))
        for h in heads:
            s = state[h]
            res = _dot(jnp.concatenate([kuw[h][:, HEAD_DIM:], qp[h][rows]], axis=0), s)
            o = res[HEAD_DIM:] + o0[h][rows]
            state[h] = s * dlast[h] - res[:HEAD_DIM] + kuw[h][:, :HEAD_DIM]
            o_ref[rows, h * HEAD_DIM:(h + 1) * HEAD_DIM] = _rms(o).astype(BF16)
    for h in heads:
        st_ref[h] = state[h]


def _gdn(p3, small3, conv_w, alog, dtb, tc):
    b, s, _ = p3.shape
    w = HEADS * HEAD_DIM
    return pl.pallas_call(
        functools.partial(_gdn_kernel, tc=tc),
        grid=(b, s // tc),
        in_specs=[
            pl.BlockSpec((None, tc, w), lambda i, j: (i, j, 0)),
            pl.BlockSpec((None, tc, w), lambda i, j: (i, j, 1)),
            pl.BlockSpec((None, tc, w), lambda i, j: (i, j, 2)),
            pl.BlockSpec((None, tc, LANES), lambda i, j: (i, j, 0)),
            pl.BlockSpec((CONV_K, 3 * w), lambda i, j: (0, 0)),
            pl.BlockSpec((1, LANES), lambda i, j: (0, 0)),
            pl.BlockSpec((1, LANES), lambda i, j: (0, 0)),
        ],
        out_specs=pl.BlockSpec((None, tc, w), lambda i, j: (i, j, 0)),
        out_shape=jax.ShapeDtypeStruct((b, s, w), BF16),
        scratch_shapes=[pltpu.VMEM((16, 3 * w), F32), pltpu.VMEM((HEADS, HEAD_DIM, HEAD_DIM), F32)],
        compiler_params=_cparams(2),
        name="gdn",
    )(p3, p3, p3, small3, conv_w, alog, dtb)


def _hgrn_kernel(f_ref, q_ref, i_ref, lb_ref, o_ref, st_ref, *, tc):
    nc = tc // CHUNK

    @pl.when(pl.program_id(1) == 0)
    def _():
        st_ref[...] = jnp.zeros_like(st_ref)

    _, _, causal, _ = _chunk_masks(tc)
    tri = jnp.where(causal, 1.0, 0.0).astype(BF16)

    lbp = lb_ref[...]
    lbe = jnp.exp(lbp - jnp.max(lbp, axis=0, keepdims=True))
    lb = lbe[0:1, :] / jnp.sum(lbe, axis=0, keepdims=True)

    heads = range(HEADS)
    hcols = [slice(h * HEAD_DIM, (h + 1) * HEAD_DIM) for h in heads]
    forget = lb + (1.0 - lb) * jax.nn.sigmoid(f_ref[...].astype(F32))
    cum = _dot_exact_lhs(tri, jnp.log(forget))
    kk = 1.0 - forget
    q_in = (q_ref[...].astype(F32) * HEAD_DIM ** -0.5 * jnp.exp(cum)).astype(BF16)
    k_in = (kk * jnp.exp(-cum)).astype(BF16)
    v = i_ref[...]
    intra = [jnp.where(causal, _dot_nt(q_in[:, hc], k_in[:, hc]), 0.0).astype(BF16) for hc in hcols]
    state = [st_ref[h] for h in heads]
    for c in range(nc):
        r0 = c * CHUNK
        rows = slice(r0, r0 + CHUNK)
        last = cum[r0 + CHUNK - 1:r0 + CHUNK, :]
        k_dec = (kk[rows] * jnp.exp(last - cum[rows])).astype(BF16)
        dlast = jnp.exp(last)
        for h, hc in zip(heads, hcols):
            o = _dot_nt(q_in[rows, hc], state[h]) + _dot(intra[h][rows, rows], v[rows, hc])
            state[h] = state[h] * dlast[:, hc] + _dot_tn(v[rows, hc], k_dec[:, hc])
            o_ref[rows, hc] = _rms(o).astype(BF16)
    for h in heads:
        st_ref[h] = state[h]


def _hgrn(p3, lb_logits, tc):
    b, s, _ = p3.shape
    w = HEADS * HEAD_DIM
    return pl.pallas_call(
        functools.partial(_hgrn_kernel, tc=tc),
        grid=(b, s // tc),
        in_specs=[
            pl.BlockSpec((None, tc, w), lambda i, j: (i, j, 4)),
            pl.BlockSpec((None, tc, w), lambda i, j: (i, j, 5)),
            pl.BlockSpec((None, tc, w), lambda i, j: (i, j, 6)),
            pl.BlockSpec(lb_logits.shape, lambda i, j: (0, 0)),
        ],
        out_specs=pl.BlockSpec((None, tc, w), lambda i, j: (i, j, 0)),
        out_shape=jax.ShapeDtypeStruct((b, s, w), BF16),
        scratch_shapes=[pltpu.VMEM((HEADS, HEAD_DIM, HEAD_DIM), F32)],
        compiler_params=_cparams(2),
        name="hgrn",
    )(p3, p3, p3, lb_logits)


def _kv_kernel(mem_ref, nw_ref, wkv_ref, kv_ref):
    mn = (_rms(mem_ref[...]) * nw_ref[...]).astype(BF16)
    kv_ref[...] = jnp.dot(mn, wkv_ref[...], preferred_element_type=F32).astype(BF16)


def _kv(mem, nw, wkv):
    b, m, d = mem.shape
    return pl.pallas_call(
        _kv_kernel,
        grid=(b,),
        in_specs=[
            pl.BlockSpec((None, m, d), lambda i: (i, 0, 0)),
            pl.BlockSpec((1, d), lambda i: (0, 0)),
            pl.BlockSpec(wkv.shape, lambda i: (0, 0)),
        ],
        out_specs=pl.BlockSpec((None, m, 2 * d), lambda i: (i, 0, 0)),
        out_shape=jax.ShapeDtypeStruct((b, m, 2 * d), BF16),
        compiler_params=_cparams(1),
        name="kv",
    )(mem, nw, wkv)


def _post_kernel(x_ref, oa_ref, ob_ref, oga_ref, ogb_ref, ga_ref, gb_ref, kv_ref,
                 gnw_ref, hnw_ref, wa_ref, wb_ref, wout_ref, nx_ref, wq_ref, wo_ref, nf_ref,
                 wr_ref, br_ref, h2_ref, hn3_ref, lg_ref):
    d = x_ref.shape[-1]
    dh = d // XA_HEADS
    ya = oa_ref[...].astype(F32) * gnw_ref[...] * _silu(oga_ref[...].astype(F32))
    yb = ob_ref[...].astype(F32) * hnw_ref[...] * _silu(ogb_ref[...].astype(F32))
    merged = (jax.nn.sigmoid(ga_ref[...].astype(F32)) * _dot(ya, wa_ref[...])
              + jax.nn.sigmoid(gb_ref[...].astype(F32)) * _dot(yb, wb_ref[...]))
    h1 = x_ref[...] + _dot(merged, wout_ref[...])

    q = _dot(_rms(h1) * nx_ref[...], wq_ref[...]) * dh ** -0.5
    outs = []
    for hh in range(XA_HEADS):
        kh = kv_ref[:, hh * dh:(hh + 1) * dh]
        vh = kv_ref[:, d + hh * dh:d + (hh + 1) * dh]
        sc = _dot_nt(q[:, hh * dh:(hh + 1) * dh], kh)
        p = jnp.exp(sc - jnp.max(sc, axis=-1, keepdims=True))
        outs.append(_dot(p, vh) / jnp.sum(p, axis=-1, keepdims=True))
    h2 = h1 + _dot(jnp.concatenate(outs, axis=1), wo_ref[...])
    h2_ref[...] = h2

    hn3 = _rms(h2) * nf_ref[...]
    _to_slab(hn3_ref, hn3)
    hi, lo = _split(hn3)
    whi, wlo = _split(wr_ref[...])
    hw = jnp.dot(hi, jnp.concatenate([whi, wlo], axis=1), preferred_element_type=F32)
    lg_ref[...] = (hw[:, :LANES] + hw[:, LANES:]
                   + jnp.dot(lo, whi, preferred_element_type=F32)) + br_ref[...]


def _post(x3, oa, ob, p3, kv, gnw, hnw, wa, wb, wout, nx, wq, wo, nf, wr, br, tm):
    b, s, d = x3.shape
    ns = _slab_rows_per_token(d)
    row = lambda c: pl.BlockSpec((None, tm, d), lambda i, j: (i, j, c))
    full = lambda a: pl.BlockSpec(a.shape, lambda i, j: (0,) * a.ndim, pipeline_mode=pl.Buffered(1))
    return pl.pallas_call(
        _post_kernel,
        grid=(b, s // tm),
        in_specs=[row(0), row(0), row(0), row(3), row(7), row(8), row(9),
                  pl.BlockSpec((None,) + kv.shape[1:], lambda i, j: (i, 0, 0)),
                  full(gnw), full(hnw), full(wa), full(wb), full(wout), full(nx), full(wq), full(wo),
                  full(nf), full(wr), full(br)],
        out_specs=[row(0), pl.BlockSpec((tm * ns, LANES), lambda i, j: (i * (s // tm) + j, 0)),
                   pl.BlockSpec((None, tm, LANES), lambda i, j: (i, j, 0))],
        out_shape=[jax.ShapeDtypeStruct((b, s, d), F32), jax.ShapeDtypeStruct((b * s * ns, LANES), U32),
                   jax.ShapeDtypeStruct((b, s, LANES), F32)],
        compiler_params=_cparams(2),
        name="post",
    )(x3, oa, ob, p3, p3, p3, p3, kv, gnw, hnw, wa, wb, wout, nx, wq, wo, nf, wr, br)


def _route_kernel(lg_ref, code_ref, wt_ref, cnt_ref, carry_ref, *, tr):
    @pl.when(pl.program_id(0) == 0)
    def _():
        carry_ref[...] = jnp.zeros_like(carry_ref)

    neg = -1e30
    big = 2 * LANES
    lg = lg_ref[...]
    lane = lax.broadcasted_iota(I32, (tr, LANES), 1)
    lane_f = lane.astype(F32)
    first = lambda m: jnp.min(jnp.where(m, lane_f, big), axis=-1, keepdims=True).astype(I32)

    is_g = lane < N_GROUPS
    gl = jnp.where(is_g, lg, neg)
    gmax = jnp.max(gl, axis=-1, keepdims=True)
    gidx = first(gl == gmax)
    g_p = 1.0 / jnp.sum(jnp.where(is_g, jnp.exp(gl - gmax), 0.0), axis=-1, keepdims=True)

    lo_lane = N_GROUPS + gidx * EXP_PER_GROUP
    in_grp = (lane >= lo_lane) & (lane < lo_lane + EXP_PER_GROUP)
    el = jnp.where(in_grp, lg, neg)
    m1 = jnp.max(el, axis=-1, keepdims=True)
    i1 = first(el == m1)
    el2 = jnp.where(lane == i1, neg, el)
    m2 = jnp.max(el2, axis=-1, keepdims=True)
    i2 = first(el2 == m2)
    esum = jnp.sum(jnp.where(in_grp, jnp.exp(el - m1), 0.0), axis=-1, keepdims=True)
    p1 = 1.0 / esum
    p2 = jnp.exp(m2 - m1) / esum
    w1 = g_p * p1 / (p1 + p2)
    w2 = g_p * p2 / (p1 + p2)
    e1 = i1 - N_GROUPS
    e2 = i2 - N_GROUPS

    oh1 = lane == e1
    oh2 = lane == e2
    both = jnp.where(oh1 | oh2, 1.0, 0.0)
    sub = min(tr, MOE_BLOCK)
    ri = lax.broadcasted_iota(I32, (sub, sub), 0)
    ci = lax.broadcasted_iota(I32, (sub, sub), 1)
    below = jnp.where(ri > ci, 1.0, 0.0).astype(BF16)
    carry = carry_ref[...]
    r1, r2 = [], []
    for sb in range(tr // sub):
        rows = slice(sb * sub, (sb + 1) * sub)
        before = jnp.dot(below, both[rows].astype(BF16), preferred_element_type=F32) + carry
        r1.append(jnp.sum(jnp.where(oh1[rows], before, 0.0), axis=-1, keepdims=True))
        r2.append(jnp.sum(jnp.where(oh2[rows], before, 0.0), axis=-1, keepdims=True))
        carry = carry + jnp.sum(both[rows], axis=0, keepdims=True)
    r1 = jnp.concatenate(r1, axis=0).astype(I32)
    r2 = jnp.concatenate(r2, axis=0).astype(I32)
    carry_ref[...] = carry

    code_ref[...] = jnp.where(lane == 0, e1 * (1 << RANK_BITS) + r1,
                              jnp.where(lane == 1, e2 * (1 << RANK_BITS) + r2, 0))
    wt_ref[...] = jnp.where(lane == 0, w1, jnp.where(lane == 1, w2, 0.0))
    cnt_ref[...] = carry_ref[...]


def _route(logits, tr):
    t = logits.shape[0]
    return pl.pallas_call(
        functools.partial(_route_kernel, tr=tr),
        grid=(t // tr,),
        in_specs=[pl.BlockSpec((tr, LANES), lambda i: (i, 0))],
        out_specs=[pl.BlockSpec((tr, LANES), lambda i: (i, 0)),
                   pl.BlockSpec((tr, LANES), lambda i: (i, 0)),
                   pl.BlockSpec((1, LANES), lambda i: (0, 0))],
        out_shape=[jax.ShapeDtypeStruct((t, LANES), I32), jax.ShapeDtypeStruct((t, LANES), F32),
                   jax.ShapeDtypeStruct((1, LANES), F32)],
        scratch_shapes=[pltpu.VMEM((1, LANES), F32)],
        compiler_params=_cparams(1),
        name="route",
    )(logits)


def _dest_kernel(code_ref, ps_ref, out_ref):
    code = code_ref[...]
    lane = lax.broadcasted_iota(I32, code.shape, 1)
    ps = ps_ref[...]

    def dest(slot):
        c = code[:, slot:slot + 1]
        start = jnp.sum(jnp.where(lane == (c >> RANK_BITS), ps, 0.0), axis=-1, keepdims=True)
        return (c & ((1 << RANK_BITS) - 1)).astype(F32) + start

    both = jnp.where(lane == 0, dest(0), jnp.where(lane == 1, dest(1), 0.0))
    out_ref[...] = both.T[0:8, :].astype(I32)


def _dest(codes, pstart_lanes, tr):
    t = codes.shape[0]
    out = pl.pallas_call(
        _dest_kernel,
        grid=(t // tr,),
        in_specs=[pl.BlockSpec((tr, LANES), lambda i: (i, 0)),
                  pl.BlockSpec((1, LANES), lambda i: (0, 0))],
        out_specs=pl.BlockSpec((None, 8, tr), lambda i: (i, 0, 0)),
        out_shape=jax.ShapeDtypeStruct((t // tr, 8, tr), I32),
        compiler_params=_cparams(1),
        name="dest",
    )(codes, pstart_lanes)
    return out[:, 0, :].reshape(t), out[:, 1, :].reshape(t)


ISSUE_UNROLL = 8


def _slab_rows(ref, row, ns):
    return ref.at[pl.ds(pl.multiple_of(row * ns, ns), ns)]


def _dispatch_kernel(ps_ref, sz_ref, d1_ref, d2_ref, hn_ref, wg_ref, wu_ref, wd_ref,
                     xp_ref, wgb_ref, wub_ref, wdb_ref, zero_ref, sem, *, tr, ns):
    step = pl.program_id(0)
    base = step * tr

    def copy(i, d):
        return pltpu.make_async_copy(_slab_rows(hn_ref, i, ns), _slab_rows(xp_ref, d, ns), sem)

    def pad_copy(d):
        return pltpu.make_async_copy(zero_ref.at[pl.ds(0, ns)], _slab_rows(xp_ref, d, ns), sem)

    def pad_block_copy(blk):
        return pltpu.make_async_copy(zero_ref, _slab_rows(xp_ref, blk, MOE_BLOCK * ns), sem)

    def issue(g, carry):
        for u in range(ISSUE_UNROLL):
            i = g * ISSUE_UNROLL + u
            copy(i, d1_ref[base + i]).start(priority=0)
            copy(i, d2_ref[base + i]).start(priority=1)
        return carry

    def drain(g, carry):
        for _ in range(2 * ISSUE_UNROLL):
            copy(0, 0).wait()
        return carry

    lax.fori_loop(0, tr // ISSUE_UNROLL, issue, 0)
    wgb_ref[...] = wg_ref[...].astype(BF16)
    wub_ref[...] = wu_ref[...].astype(BF16)
    wdb_ref[...] = wd_ref[...].astype(BF16)
    lax.fori_loop(0, tr // ISSUE_UNROLL, drain, 0)

    @pl.when(step == pl.num_programs(0) - 1)
    def _():
        zero_ref[...] = jnp.zeros_like(zero_ref)

        def pad_expert(e, n_pad):
            n = sz_ref[e]
            first = ps_ref[e] + n
            n_e = (MOE_BLOCK - n % MOE_BLOCK) % MOE_BLOCK

            def one(r, c):
                pad_copy(first + r).start()
                return c

            lax.fori_loop(0, n_e, one, 0)
            return n_pad + n_e

        n_pad = lax.fori_loop(0, N_EXPERTS, pad_expert, 0)

        def drain_pad(r, c):
            pad_copy(0).wait()
            return c

        lax.fori_loop(0, n_pad, drain_pad, 0)

        first_blk = (ps_ref[N_EXPERTS - 1] + sz_ref[N_EXPERTS - 1] + MOE_BLOCK - 1) // MOE_BLOCK
        n_blocks = xp_ref.shape[0] // (MOE_BLOCK * ns)

        def tail_start(blk, c):
            pad_block_copy(blk).start()
            return c

        def tail_wait(blk, c):
            pad_block_copy(blk).wait()
            return c

        lax.fori_loop(first_blk, n_blocks, tail_start, 0)
        lax.fori_loop(first_blk, n_blocks, tail_wait, 0)


def _dispatch(pstart, sizes, dest1, dest2, hn3_slab, wg, wu, wd, n_rows, ns):
    t = dest1.shape[0]
    n_e, d, ff = wg.shape
    tr = t // n_e
    assert tr * n_e == t and tr % ISSUE_UNROLL == 0
    per_expert = lambda r, c: pl.BlockSpec((None, r, c), lambda i, *_: (i, 0, 0))
    return pl.pallas_call(
        functools.partial(_dispatch_kernel, tr=tr, ns=ns),
        grid_spec=pltpu.PrefetchScalarGridSpec(
            num_scalar_prefetch=4,
            grid=(n_e,),
            in_specs=[pl.BlockSpec((tr * ns, LANES), lambda i, *_: (i, 0)),
                      per_expert(d, ff), per_expert(d, ff), per_expert(ff, d)],
            out_specs=[pl.BlockSpec(memory_space=pl.ANY),
                       per_expert(d, ff), per_expert(d, ff), per_expert(ff, d)],
            scratch_shapes=[pltpu.VMEM((MOE_BLOCK * ns, LANES), U32), pltpu.SemaphoreType.DMA],
        ),
        out_shape=[jax.ShapeDtypeStruct((n_rows * ns, LANES), U32),
                   jax.ShapeDtypeStruct(wg.shape, BF16), jax.ShapeDtypeStruct(wu.shape, BF16),
                   jax.ShapeDtypeStruct(wd.shape, BF16)],
        compiler_params=_cparams(1),
        name="dispatch",
    )(pstart, sizes, dest1, dest2, hn3_slab, wg, wu, wd)


BLOCKS_PER_STEP = 2


def _expert_kernel(be_ref, nu_ref, x_ref, *refs):
    del be_ref
    w_refs, y_ref = refs[:-1], refs[-1]
    rows = y_ref.shape[0] // BLOCKS_PER_STEP
    n_live = nu_ref[0] - pl.program_id(0) * BLOCKS_PER_STEP

    def mlp(j):
        wg_ref, wu_ref, wd_ref = w_refs[3 * j:3 * j + 3]
        x = _from_slab(x_ref.at[pl.ds(j * rows, rows)], MOE_BLOCK).astype(BF16)
        hmid = _silu(_dot(x, wg_ref[...])) * _dot(x, wu_ref[...])
        return _dot(hmid, wd_ref[...])

    for live in range(BLOCKS_PER_STEP + 1):
        if live == 0:
            cond = n_live <= 0
        elif live == BLOCKS_PER_STEP:
            cond = n_live >= live
        else:
            cond = n_live == live

        @pl.when(cond)
        def _(live=live):
            ys = [mlp(j) for j in range(live)]
            for j in range(BLOCKS_PER_STEP):
                out = y_ref.at[pl.ds(j * rows, rows)]
                if j < live:
                    _to_slab(out, ys[j])
                else:
                    out[...] = jnp.zeros_like(out)


def _experts(blk_e, n_used, x_pad, wg, wu, wd):
    d, ff = wg.shape[-2:]
    blk = MOE_BLOCK * _slab_rows_per_token(d)
    n_steps = x_pad.shape[0] // (blk * BLOCKS_PER_STEP)
    assert n_steps * blk * BLOCKS_PER_STEP == x_pad.shape[0]
    rows = lambda i, be, nu: (jnp.minimum(i, (nu[0] - 1) // BLOCKS_PER_STEP), 0)
    weights = lambda j, r, c: pl.BlockSpec((None, r, c), lambda i, be, nu: (be[i * BLOCKS_PER_STEP + j], 0, 0))
    w_specs, w_args = [], []
    for j in range(BLOCKS_PER_STEP):
        w_specs += [weights(j, d, ff), weights(j, d, ff), weights(j, ff, d)]
        w_args += [wg, wu, wd]
    return pl.pallas_call(
        _expert_kernel,
        grid_spec=pltpu.PrefetchScalarGridSpec(
            num_scalar_prefetch=2,
            grid=(n_steps,),
            in_specs=[pl.BlockSpec((blk * BLOCKS_PER_STEP, LANES), rows)] + w_specs,
            out_specs=pl.BlockSpec((blk * BLOCKS_PER_STEP, LANES), lambda i, be, nu: (i, 0)),
        ),
        out_shape=jax.ShapeDtypeStruct(x_pad.shape, U32),
        compiler_params=_cparams(1),
        name="experts",
    )(blk_e, n_used, x_pad, *w_args)


def _combine_kernel(d1_ref, d2_ref, h2_ref, wt_ref, fnw_ref, y_ref, out_ref, b1_ref, b2_ref, sem, *, tr, ns):
    step = pl.program_id(0)
    n_steps = pl.num_programs(0)

    def copy(d, buf, slot, i):
        return pltpu.make_async_copy(_slab_rows(y_ref, d, ns), _slab_rows(buf.at[slot], i, ns), sem.at[slot])

    def issue_step(st):
        slot = st % 2

        def issue(g, carry):
            for u in range(ISSUE_UNROLL):
                i = g * ISSUE_UNROLL + u
                copy(d1_ref[st * tr + i], b1_ref, slot, i).start(priority=0)
                copy(d2_ref[st * tr + i], b2_ref, slot, i).start(priority=1)
            return carry

        lax.fori_loop(0, tr // ISSUE_UNROLL, issue, 0)

    @pl.when(step == 0)
    def _():
        issue_step(step)

    @pl.when(step + 1 < n_steps)
    def _():
        issue_step(step + 1)

    slot = step % 2

    def drain(g, carry):
        for _ in range(ISSUE_UNROLL):
            copy(0, b1_ref, slot, 0).wait()
            copy(0, b2_ref, slot, 0).wait()
        return carry

    lax.fori_loop(0, tr // ISSUE_UNROLL, drain, 0)

    wt = wt_ref[...]
    h3 = (h2_ref[...] + wt[:, 0:1] * _from_slab(b1_ref.at[slot], tr)
          + wt[:, 1:2] * _from_slab(b2_ref.at[slot], tr))
    out_ref[...] = _rms(h3) * fnw_ref[...]


def _combine(dest1, dest2, h2, wts, fnw, y_pad, tr):
    t, d = h2.shape
    ns = _slab_rows_per_token(d)
    return pl.pallas_call(
        functools.partial(_combine_kernel, tr=tr, ns=ns),
        grid_spec=pltpu.PrefetchScalarGridSpec(
            num_scalar_prefetch=2,
            grid=(t // tr,),
            in_specs=[pl.BlockSpec((tr, d), lambda i, *_: (i, 0)),
                      pl.BlockSpec((tr, LANES), lambda i, *_: (i, 0)),
                      pl.BlockSpec((1, d), lambda i, *_: (0, 0)),
                      pl.BlockSpec(memory_space=pl.ANY)],
            out_specs=pl.BlockSpec((tr, d), lambda i, *_: (i, 0)),
            scratch_shapes=[pltpu.VMEM((2, tr * ns, LANES), U32), pltpu.VMEM((2, tr * ns, LANES), U32),
                            pltpu.SemaphoreType.DMA((2,))],
        ),
        out_shape=jax.ShapeDtypeStruct((t, d), F32),
        compiler_params=_cparams(1),
        name="combine",
    )(dest1, dest2, h2, wts, fnw, y_pad)


def _pick(n, pref):
    while n % pref:
        pref //= 2
    return pref


def kernel(x, mem, norm_mix_w, w_in, conv_w, gdn_a_log, gdn_dt_bias, gdn_out_norm_w, hgrn_lb, hgrn_out_norm_w, w_branch_a, w_branch_b, w_out, norm_xattn_w, norm_mem_w, xattn_wq, xattn_wkv, xattn_wo, norm_ffn_w, router_group_w, router_group_b, router_expert_w, router_expert_b, expert_w_gate, expert_w_up, expert_w_down, final_norm_w):
    b, s, d = x.shape
    t = b * s
    depth = w_in.shape[0]
    w = HEADS * HEAD_DIM
    qkv_w = 3 * w
    assert d == w and s % CHUNK == 0

    tc = _pick(s, 256)
    tm_proj = _pick(t, 1024)
    tm_post = _pick(s, 512)
    tr = _pick(t, 256)

    def pad_lanes(v, offset=0):
        return jnp.zeros((1, LANES), F32).at[0, offset:offset + v.shape[0]].set(v.astype(F32))

    assert depth == 1
    h3d = x
    for layer in range(depth):
        wl = w_in[layer]
        w_main = jnp.concatenate([wl[:, :qkv_w], wl[:, qkv_w + 2 * HEADS:]], axis=1).astype(BF16)
        w_small = jnp.zeros((d, LANES), F32).at[:, :2 * HEADS].set(wl[:, qkv_w:qkv_w + 2 * HEADS]).astype(BF16)
        p_main, p_small = _proj(h3d.reshape(t, d), norm_mix_w[layer][None, :], w_main, w_small,
                                tm_proj, 2048)
        p3 = p_main.reshape(b, s, -1)
        o_a = _gdn(p3, p_small.reshape(b, s, LANES), conv_w[layer], pad_lanes(gdn_a_log[layer]),
                   pad_lanes(gdn_dt_bias[layer]), tc)
        o_b = _hgrn(p3, hgrn_lb[layer:], tc)
        kv = _kv(mem, norm_mem_w[layer][None, :], xattn_wkv[layer].astype(BF16))
        w_router = jnp.zeros((d, LANES), F32)
        w_router = w_router.at[:, :N_GROUPS].set(router_group_w[layer])
        w_router = w_router.at[:, N_GROUPS:N_GROUPS + N_EXPERTS].set(router_expert_w[layer])
        b_router = pad_lanes(router_group_b[layer]) + pad_lanes(router_expert_b[layer], N_GROUPS)
        tile8 = lambda v: jnp.tile(v.astype(F32), HEADS)[None, :]
        h2, hn3, logits = _post(
            h3d, o_a, o_b, p3, kv, tile8(gdn_out_norm_w[layer]), tile8(hgrn_out_norm_w[layer]),
            w_branch_a[layer].astype(BF16), w_branch_b[layer].astype(BF16), w_out[layer].astype(BF16),
            norm_xattn_w[layer][None, :], xattn_wq[layer].astype(BF16), xattn_wo[layer].astype(BF16),
            norm_ffn_w[layer][None, :], w_router, b_router, tm_post)

        tr_route = _pick(t, 1024)
        codes, wts, counts = _route(logits.reshape(t, LANES), tr_route)
        sizes = counts[0, :N_EXPERTS].astype(I32)
        padded = ((sizes + MOE_BLOCK - 1) // MOE_BLOCK) * MOE_BLOCK
        pend = jnp.cumsum(padded)
        pstart = (pend - padded).astype(I32)
        m = t * 2
        n_rows = ((m + MOE_BLOCK - 1) // MOE_BLOCK) * MOE_BLOCK + N_EXPERTS * MOE_BLOCK
        n_blocks = -(-(n_rows // MOE_BLOCK) // BLOCKS_PER_STEP) * BLOCKS_PER_STEP
        n_rows = n_blocks * MOE_BLOCK
        blk_start = jnp.arange(n_blocks, dtype=I32) * MOE_BLOCK
        blk_e = jnp.minimum(jnp.sum(pend[None, :] <= blk_start[:, None], axis=1), N_EXPERTS - 1).astype(I32)
        n_used = (pend[-1:] // MOE_BLOCK).astype(I32)
        dest1, dest2 = _dest(codes, pad_lanes(pstart), tr_route)

        x_pad, wg, wu, wd = _dispatch(pstart, sizes, dest1, dest2, hn3, expert_w_gate[layer], expert_w_up[layer],
                                      expert_w_down[layer], n_rows, _slab_rows_per_token(d))
        y_pad = _experts(blk_e, n_used, x_pad, wg, wu, wd)
        out = _combine(dest1, dest2, h2.reshape(t, d), wts, final_norm_w[None, :], y_pad, tr)
        h3d = out.reshape(b, s, d)
    return h3d
```

```python
import functools

import jax
import jax.numpy as jnp
from jax import lax
from jax.experimental import pallas as pl
from jax.experimental.pallas import tpu as pltpu

F32 = jnp.float32
BF16 = jnp.bfloat16
I32 = jnp.int32
U32 = jnp.uint32

EPS = 1e-6
CHUNK = 64
HEADS = 8
HEAD_DIM = 128
CONV_K = 4
XA_HEADS = 4
N_GROUPS = 4
EXP_PER_GROUP = 8
N_EXPERTS = N_GROUPS * EXP_PER_GROUP
MOE_BLOCK = 256
LANES = 128

VMEM_LIMIT = 52 * 1024 * 1024


def _cparams(n_axes):
    return pltpu.CompilerParams(dimension_semantics=("arbitrary",) * n_axes,
                                vmem_limit_bytes=VMEM_LIMIT)


def _dot(a, b):
    return jnp.dot(a.astype(BF16), b.astype(BF16), preferred_element_type=F32)


def _dot_nt(a, b):
    return lax.dot_general(a.astype(BF16), b.astype(BF16), (((1,), (1,)), ((), ())),
                           preferred_element_type=F32)


def _dot_tn(a, b):
    return lax.dot_general(a.astype(BF16), b.astype(BF16), (((0,), (0,)), ((), ())),
                           preferred_element_type=F32)


def _split(x):
    hi = x.astype(BF16)
    lo = (x - hi.astype(F32)).astype(BF16)
    return hi, lo


def _dot_exact_lhs(m_bf16, x):
    hi, lo = _split(x)
    return (jnp.dot(m_bf16, hi, preferred_element_type=F32)
            + jnp.dot(m_bf16, lo, preferred_element_type=F32))


def _rms(x):
    return x * lax.rsqrt(jnp.mean(x * x, axis=-1, keepdims=True) + EPS)


def _silu(x):
    return x * jax.nn.sigmoid(x)


def _softplus(x):
    return jnp.maximum(x, 0.0) + jnp.log(1.0 + jnp.exp(-jnp.abs(x)))


HIGH_HALF = 0xFFFF0000


def _slab_rows_per_token(d):
    return d // (2 * LANES)


def _to_slab(ref, x):
    n, d = x.shape
    ns = _slab_rows_per_token(d)
    bits = lambda v: pltpu.bitcast(v.astype(BF16).astype(F32), U32)
    for s in range(ns):
        lo = bits(x[:, s * LANES:(s + 1) * LANES])
        hi = bits(x[:, (s + ns) * LANES:(s + ns + 1) * LANES])
        ref[pl.ds(s, n, stride=ns), :] = (lo >> 16) | (hi & jnp.uint32(HIGH_HALF))


def _from_slab(ref, n):
    ns = ref.shape[0] // n
    words = [ref[pl.ds(s, n, stride=ns), :] for s in range(ns)]
    lo = [pltpu.bitcast(wd << 16, F32) for wd in words]
    hi = [pltpu.bitcast(wd & jnp.uint32(HIGH_HALF), F32) for wd in words]
    return jnp.concatenate(lo + hi, axis=1)


def _chunk_masks(tc):
    ri = lax.broadcasted_iota(I32, (tc, tc), 0)
    ci = lax.broadcasted_iota(I32, (tc, tc), 1)
    same = (ri // CHUNK) == (ci // CHUNK)
    causal = same & (ri >= ci)
    strict = same & (ri > ci)
    return ri, ci, causal, strict


def _proj_kernel(x_ref, nw_ref, w_ref, ws_ref, out_ref, small_ref, hn_ref, *, n_small):
    @pl.when(pl.program_id(1) == 0)
    def _():
        hn = (_rms(x_ref[...]) * nw_ref[...]).astype(BF16)
        hn_ref[...] = hn
        lane = lax.broadcasted_iota(I32, ws_ref.shape, 1)
        ws = jnp.where(lane < n_small, ws_ref[...], 0.0).astype(BF16)
        small_ref[...] = jnp.dot(hn, ws, preferred_element_type=F32)

    out_ref[...] = jnp.dot(hn_ref[...], w_ref[...], preferred_element_type=F32).astype(BF16)


def _proj(x2, nw, w_main, w_all, small_col, n_small, tm, tn):
    t, d = x2.shape
    n = w_main.shape[1]
    assert small_col % LANES == 0 and small_col + LANES <= w_all.shape[1]
    return pl.pallas_call(
        functools.partial(_proj_kernel, n_small=n_small),
        grid=(t // tm, n // tn),
        in_specs=[
            pl.BlockSpec((tm, d), lambda i, j: (i, 0)),
            pl.BlockSpec((1, d), lambda i, j: (0, 0)),
            pl.BlockSpec((d, tn), lambda i, j: (0, j)),
            pl.BlockSpec((d, LANES), lambda i, j: (0, small_col // LANES)),
        ],
        out_specs=[
            pl.BlockSpec((tm, tn), lambda i, j: (i, j)),
            pl.BlockSpec((tm, LANES), lambda i, j: (i, 0)),
        ],
        out_shape=[jax.ShapeDtypeStruct((t, n), BF16), jax.ShapeDtypeStruct((t, LANES), F32)],
        scratch_shapes=[pltpu.VMEM((tm, d), BF16)],
        compiler_params=_cparams(2),
        name="proj",
    )(x2, nw, w_main, w_all)


def _gdn_kernel(q_ref, k_ref, v_ref, sm_ref, cw_ref, alog_ref, dtb_ref, o_ref, xs_ref, st_ref, *, tc):
    w = HEADS * HEAD_DIM
    nc = tc // CHUNK

    @pl.when(pl.program_id(1) == 0)
    def _():
        xs_ref[0:8, :] = jnp.zeros((8, 3 * w), F32)
        st_ref[...] = jnp.zeros_like(st_ref)

    _, _, causal, strict = _chunk_masks(tc)
    tri = jnp.where(causal, 1.0, 0.0).astype(BF16)
    wi = lax.broadcasted_iota(I32, (CHUNK, tc), 0)
    wj = lax.broadcasted_iota(I32, (CHUNK, tc), 1)
    eye_w = jnp.where(wi == wj % CHUNK, 1.0, 0.0)
    blk_w = wj // CHUNK

    def fold(m_bd):
        acc = m_bd[0:CHUNK]
        for c in range(1, nc):
            acc = acc + m_bd[c * CHUNK:(c + 1) * CHUNK]
        return acc

    def unfold(m_w):
        return jnp.concatenate([jnp.where(blk_w == c, m_w, 0.0) for c in range(nc)], axis=0)

    sm = sm_ref[...]
    lane = lax.broadcasted_iota(I32, (tc, LANES), 1)
    g_all = jnp.where(lane < HEADS, -jnp.exp(alog_ref[...]) * _softplus(sm + dtb_ref[...]), 0.0)
    beta_all = jax.nn.sigmoid(sm)
    cum = _dot_exact_lhs(tri, g_all)
    ecum = jnp.exp(cum)
    cum_t = cum.T

    sr = lax.broadcasted_iota(I32, ((CONV_K - 1) * tc, tc), 0)
    sc = lax.broadcasted_iota(I32, ((CONV_K - 1) * tc, tc), 1)
    assert tc & (tc - 1) == 0
    shifts = jnp.where(sc == (sr & (tc - 1)) - ((sr >> (tc.bit_length() - 1)) + 1), 1.0, 0.0).astype(BF16)

    def conv_part(p, ref):
        cols = slice(p * w, (p + 1) * w)
        xb = ref[...]
        shifted = jnp.dot(shifts, xb, preferred_element_type=F32)
        acc = cw_ref[CONV_K - 1:CONV_K, cols] * xb.astype(F32)
        for s in range(1, CONV_K):
            acc = acc + cw_ref[CONV_K - 1 - s:CONV_K - s, cols] * shifted[(s - 1) * tc:s * tc]
        xs_ref[8:16, cols] = xb[0:8].astype(F32)
        first = cw_ref[CONV_K - 1:CONV_K, cols] * xs_ref[8:16, cols]
        for s in range(1, CONV_K):
            first = first + cw_ref[CONV_K - 1 - s:CONV_K - s, cols] * xs_ref[8 - s:16 - s, cols]
        xs_ref[0:8, cols] = xb[tc - 8:tc].astype(F32)
        return _silu(jnp.concatenate([first, acc[8:]], axis=0))

    qkv = [conv_part(p, ref) for p, ref in enumerate((q_ref, k_ref, v_ref))]

    def l2n(x):
        return x * lax.rsqrt(jnp.sum(x * x, axis=-1, keepdims=True) + EPS)

    heads = range(HEADS)
    ks, a_bd, qk_bd, rhs, qd, cc = [], [], [], [], [], []
    for h in heads:
        hc = slice(h * HEAD_DIM, (h + 1) * HEAD_DIM)
        q = l2n(qkv[0][:, hc]) * HEAD_DIM ** -0.5
        k = l2n(qkv[1][:, hc])
        v = qkv[2][:, hc]
        cch = cum[:, h:h + 1]
        cr = cum_t[h:h + 1, :]
        bc = beta_all[:, HEADS + h:HEADS + h + 1]
        ec = ecum[:, h:h + 1]
        dec = jnp.where(causal, jnp.exp(jnp.where(causal, cch - cr, 0.0)), 0.0)
        kb = k * bc
        a_bd.append(jnp.where(strict, _dot_nt(kb, k) * dec, 0.0))
        qk_bd.append(_dot_nt(q, k) * dec)
        rhs.append(jnp.concatenate([v * bc, kb * ec], axis=1).astype(BF16))
        qd.append(q * ec)
        ks.append(k)
        cc.append(cch)

    xw, pw = [], []
    for h in heads:
        aw = fold(a_bd[h])
        xw.append(eye_w - aw)
        pw.append(_dot(aw, a_bd[h]))
    n_sq = CHUNK.bit_length() - 2
    for it in range(n_sq):
        for h in heads:
            p_bd = unfold(pw[h]).astype(BF16)
            if it + 1 < n_sq:
                res = _dot(jnp.concatenate([xw[h], pw[h]], axis=0), p_bd)
                xw[h] = xw[h] + res[:CHUNK]
                pw[h] = res[CHUNK:]
            else:
                xw[h] = xw[h] + _dot(xw[h], p_bd)

    uw, qp, o0 = [], [], []
    for h in heads:
        uwh = _dot(unfold(xw[h]), rhs[h])
        qkuw = _dot(qk_bd[h], uwh)
        uw.append(uwh.astype(BF16))
        o0.append(qkuw[:, :HEAD_DIM])
        qp.append(qd[h] - qkuw[:, HEAD_DIM:])

    state = [st_ref[h] for h in heads]
    for c in range(nc):
        r0 = c * CHUNK
        rows = slice(r0, r0 + CHUNK)
        kuw, dlast = [], []
        for h in heads:
            last = cc[h][r0 + CHUNK - 1:r0 + CHUNK, :]
            kd = ks[h][rows] * jnp.exp(last - cc[h][rows])
            kuw.append(_dot_tn(kd, uw[h][rows]))
            dlast.append(jnp.exp(last))
        for h in heads:
            s = state[h]
            res = _dot(jnp.concatenate([kuw[h][:, HEAD_DIM:], qp[h][rows]], axis=0), s)
            o = res[HEAD_DIM:] + o0[h][rows]
            state[h] = s * dlast[h] - res[:HEAD_DIM] + kuw[h][:, :HEAD_DIM]
            o_ref[rows, h * HEAD_DIM:(h + 1) * HEAD_DIM] = _rms(o).astype(BF16)
    for h in heads:
        st_ref[h] = state[h]


def _gdn(p3, small3, conv_w, alog, dtb, tc):
    b, s, _ = p3.shape
    w = HEADS * HEAD_DIM
    return pl.pallas_call(
        functools.partial(_gdn_kernel, tc=tc),
        grid=(b, s // tc),
        in_specs=[
            pl.BlockSpec((None, tc, w), lambda i, j: (i, j, 0)),
            pl.BlockSpec((None, tc, w), lambda i, j: (i, j, 1)),
            pl.BlockSpec((None, tc, w), lambda i, j: (i, j, 2)),
            pl.BlockSpec((None, tc, LANES), lambda i, j: (i, j, 0)),
            pl.BlockSpec((CONV_K, 3 * w), lambda i, j: (0, 0)),
            pl.BlockSpec((1, LANES), lambda i, j: (0, 0)),
            pl.BlockSpec((1, LANES), lambda i, j: (0, 0)),
        ],
        out_specs=pl.BlockSpec((None, tc, w), lambda i, j: (i, j, 0)),
        out_shape=jax.ShapeDtypeStruct((b, s, w), BF16),
        scratch_shapes=[pltpu.VMEM((16, 3 * w), F32), pltpu.VMEM((HEADS, HEAD_DIM, HEAD_DIM), F32)],
        compiler_params=_cparams(2),
        name="gdn",
    )(p3, p3, p3, small3, conv_w, alog, dtb)


def _hgrn_kernel(f_ref, q_ref, i_ref, lb_ref, o_ref, st_ref, *, tc):
    nc = tc // CHUNK

    @pl.when(pl.program_id(1) == 0)
    def _():
        st_ref[...] = jnp.zeros_like(st_ref)

    _, _, causal, _ = _chunk_masks(tc)
    tri = jnp.where(causal, 1.0, 0.0).astype(BF16)

    lbp = lb_ref[...]
    lbe = jnp.exp(lbp - jnp.max(lbp, axis=0, keepdims=True))
    lb = lbe[0:1, :] / jnp.sum(lbe, axis=0, keepdims=True)

    heads = range(HEADS)
    hcols = [slice(h * HEAD_DIM, (h + 1) * HEAD_DIM) for h in heads]
    forget = lb + (1.0 - lb) * jax.nn.sigmoid(f_ref[...].astype(F32))
    cum = _dot_exact_lhs(tri, jnp.log(forget))
    kk = 1.0 - forget
    q_in = (q_ref[...].astype(F32) * HEAD_DIM ** -0.5 * jnp.exp(cum)).astype(BF16)
    k_in = (kk * jnp.exp(-cum)).astype(BF16)
    v = i_ref[...]
    intra = [jnp.where(causal, _dot_nt(q_in[:, hc], k_in[:, hc]), 0.0).astype(BF16) for hc in hcols]
    state = [st_ref[h] for h in heads]
    for c in range(nc):
        r0 = c * CHUNK
        rows = slice(r0, r0 + CHUNK)
        last = cum[r0 + CHUNK - 1:r0 + CHUNK, :]
        k_dec = (kk[rows] * jnp.exp(last - cum[rows])).astype(BF16)
        dlast = jnp.exp(last)
        for h, hc in zip(heads, hcols):
            o = _dot_nt(q_in[rows, hc], state[h]) + _dot(intra[h][rows, rows], v[rows, hc])
            state[h] = state[h] * dlast[:, hc] + _dot_tn(v[rows, hc], k_dec[:, hc])
            o_ref[rows, hc] = _rms(o).astype(BF16)
    for h in heads:
        st_ref[h] = state[h]


def _hgrn(p3, lb_logits, tc):
    b, s, _ = p3.shape
    w = HEADS * HEAD_DIM
    return pl.pallas_call(
        functools.partial(_hgrn_kernel, tc=tc),
        grid=(b, s // tc),
        in_specs=[
            pl.BlockSpec((None, tc, w), lambda i, j: (i, j, 4)),
            pl.BlockSpec((None, tc, w), lambda i, j: (i, j, 5)),
            pl.BlockSpec((None, tc, w), lambda i, j: (i, j, 6)),
            pl.BlockSpec(lb_logits.shape, lambda i, j: (0, 0)),
        ],
        out_specs=pl.BlockSpec((None, tc, w), lambda i, j: (i, j, 0)),
        out_shape=jax.ShapeDtypeStruct((b, s, w), BF16),
        scratch_shapes=[pltpu.VMEM((HEADS, HEAD_DIM, HEAD_DIM), F32)],
        compiler_params=_cparams(2),
        name="hgrn",
    )(p3, p3, p3, lb_logits)


def _kv_kernel(mem_ref, nw_ref, wkv_ref, kv_ref):
    mn = (_rms(mem_ref[...]) * nw_ref[...]).astype(BF16)
    kv_ref[...] = jnp.dot(mn, wkv_ref[...], preferred_element_type=F32).astype(BF16)


def _kv(mem, nw, wkv):
    b, m, d = mem.shape
    return pl.pallas_call(
        _kv_kernel,
        grid=(b,),
        in_specs=[
            pl.BlockSpec((None, m, d), lambda i: (i, 0, 0)),
            pl.BlockSpec((1, d), lambda i: (0, 0)),
            pl.BlockSpec(wkv.shape, lambda i: (0, 0)),
        ],
        out_specs=pl.BlockSpec((None, m, 2 * d), lambda i: (i, 0, 0)),
        out_shape=jax.ShapeDtypeStruct((b, m, 2 * d), BF16),
        compiler_params=_cparams(1),
        name="kv",
    )(mem, nw, wkv)


def _post_kernel(x_ref, oa_ref, ob_ref, oga_ref, ogb_ref, ga_ref, gb_ref, kv_ref,
                 gnw_ref, hnw_ref, wa_ref, wb_ref, wout_ref, nx_ref, wq_ref, wo_ref, nf_ref,
                 wr_ref, br_ref, h2_ref, hn3_ref, lg_ref):
    d = x_ref.shape[-1]
    dh = d // XA_HEADS
    ya = oa_ref[...].astype(F32) * gnw_ref[...] * _silu(oga_ref[...].astype(F32))
    yb = ob_ref[...].astype(F32) * hnw_ref[...] * _silu(ogb_ref[...].astype(F32))
    merged = (jax.nn.sigmoid(ga_ref[...].astype(F32)) * _dot(ya, wa_ref[...])
              + jax.nn.sigmoid(gb_ref[...].astype(F32)) * _dot(yb, wb_ref[...]))
    h1 = x_ref[...] + _dot(merged, wout_ref[...])

    q = _dot(_rms(h1) * nx_ref[...], wq_ref[...]) * dh ** -0.5
    outs = []
    for hh in range(XA_HEADS):
        kh = kv_ref[:, hh * dh:(hh + 1) * dh]
        vh = kv_ref[:, d + hh * dh:d + (hh + 1) * dh]
        sc = _dot_nt(q[:, hh * dh:(hh + 1) * dh], kh)
        p = jnp.exp(sc - jnp.max(sc, axis=-1, keepdims=True))
        outs.append(_dot(p, vh) / jnp.sum(p, axis=-1, keepdims=True))
    h2 = h1 + _dot(jnp.concatenate(outs, axis=1), wo_ref[...])
    h2_ref[...] = h2

    hn3 = _rms(h2) * nf_ref[...]
    _to_slab(hn3_ref, hn3)
    hi, lo = _split(hn3)
    whi, wlo = _split(wr_ref[...])
    hw = jnp.dot(hi, jnp.concatenate([whi, wlo], axis=1), preferred_element_type=F32)
    lg = (hw[:, :LANES] + hw[:, LANES:] + jnp.dot(lo, whi, preferred_element_type=F32)) + br_ref[...]
    lg_ref[...] = lg.T[0:lg_ref.shape[0], :]


def _post(x3, oa, ob, p3, kv, gnw, hnw, wa, wb, wout, nx, wq, wo, nf, wr, br, tm):
    b, s, d = x3.shape
    ns = _slab_rows_per_token(d)
    row = lambda c: pl.BlockSpec((None, tm, d), lambda i, j: (i, j, c))
    full = lambda a: pl.BlockSpec(a.shape, lambda i, j: (0,) * a.ndim, pipeline_mode=pl.Buffered(1))
    return pl.pallas_call(
        _post_kernel,
        grid=(b, s // tm),
        in_specs=[row(0), row(0), row(0), row(3), row(7), row(8), row(9),
                  pl.BlockSpec((None,) + kv.shape[1:], lambda i, j: (i, 0, 0)),
                  full(gnw), full(hnw), full(wa), full(wb), full(wout), full(nx), full(wq), full(wo),
                  full(nf), full(wr), full(br)],
        out_specs=[row(0), pl.BlockSpec((tm * ns, LANES), lambda i, j: (i * (s // tm) + j, 0)),
                   pl.BlockSpec((ROUTE_ROWS, tm), lambda i, j: (0, i * (s // tm) + j))],
        out_shape=[jax.ShapeDtypeStruct((b, s, d), F32), jax.ShapeDtypeStruct((b * s * ns, LANES), U32),
                   jax.ShapeDtypeStruct((ROUTE_ROWS, b * s), F32)],
        compiler_params=_cparams(2),
        name="post",
    )(x3, oa, ob, p3, p3, p3, p3, kv, gnw, hnw, wa, wb, wout, nx, wq, wo, nf, wr, br)


ROUTE_ROWS = 40


def _route_kernel(lg_ref, dest_ref, wt_ref, blk_ref, exp_ref, code_ref, carry_ref, *, tr, t, n_blk_lanes):
    step = pl.program_id(0)
    rr = ROUTE_ROWS

    @pl.when(step == 0)
    def _():
        carry_ref[...] = jnp.zeros_like(carry_ref)

    neg, big = -1e30, 1e9
    lt = lg_ref[...]
    row_f = lax.broadcasted_iota(I32, (rr, tr), 0).astype(F32)
    cmax = lambda v: jnp.max(v, axis=0, keepdims=True)
    csum = lambda v: jnp.sum(v, axis=0, keepdims=True)
    first = lambda m: jnp.min(jnp.where(m, row_f, big), axis=0, keepdims=True)

    is_g = row_f < N_GROUPS
    gl = jnp.where(is_g, lt, neg)
    gmax = cmax(gl)
    gidx = first(gl == gmax)
    g_p = 1.0 / csum(jnp.where(is_g, jnp.exp(gl - gmax), 0.0))

    lo = N_GROUPS + gidx * EXP_PER_GROUP
    in_grp = (row_f >= lo) & (row_f < lo + EXP_PER_GROUP)
    el = jnp.where(in_grp, lt, neg)
    m1 = cmax(el)
    i1 = first(el == m1)
    el2 = jnp.where(row_f == i1, neg, el)
    m2 = cmax(el2)
    i2 = first(el2 == m2)
    esum = csum(jnp.where(in_grp, jnp.exp(el - m1), 0.0))
    p1 = 1.0 / esum
    p2 = jnp.exp(m2 - m1) / esum
    w1 = g_p * p1 / (p1 + p2)
    w2 = g_p * p2 / (p1 + p2)

    oh1 = row_f == i1
    oh2 = row_f == i2
    both = jnp.where(oh1 | oh2, 1.0, 0.0)
    sub = min(tr, MOE_BLOCK)
    ti = lax.broadcasted_iota(I32, (sub, sub), 0)
    tj = lax.broadcasted_iota(I32, (sub, sub), 1)
    earlier = jnp.where(ti < tj, 1.0, 0.0).astype(BF16)
    carry = carry_ref[...]
    r1, r2 = [], []
    for sb in range(tr // sub):
        cols = slice(sb * sub, (sb + 1) * sub)
        before = jnp.dot(both[:, cols].astype(BF16), earlier, preferred_element_type=F32) + carry[:, 0:1]
        r1.append(csum(jnp.where(oh1[:, cols], before, 0.0)))
        r2.append(csum(jnp.where(oh2[:, cols], before, 0.0)))
        carry = carry + jnp.sum(both[:, cols], axis=1, keepdims=True)
    carry_ref[...] = carry
    r1 = jnp.concatenate(r1, axis=1)
    r2 = jnp.concatenate(r2, axis=1)

    row8 = lax.broadcasted_iota(I32, (8, tr), 0)
    rows4 = lambda a, b, c, d: jnp.where(row8 == 0, a, jnp.where(row8 == 1, b, jnp.where(row8 == 2, c,
                                                                                      jnp.where(row8 == 3, d, 0.0))))
    wt_ref[...] = rows4(w1, w2, 0.0, 0.0)
    code_ref[:, pl.ds(pl.multiple_of(step * tr, tr), tr)] = rows4(i1, i2, r1, r2)

    @pl.when(step == pl.num_programs(0) - 1)
    def _():
        blk = float(MOE_BLOCK)
        padded = jnp.floor((carry + (blk - 1.0)) * (1.0 / blk)) * blk
        ri = lax.broadcasted_iota(I32, (rr, rr), 0)
        rj = lax.broadcasted_iota(I32, (rr, rr), 1)
        lower = jnp.where(rj <= ri, 1.0, 0.0).astype(BF16)
        hi = jnp.floor(padded * (1.0 / 16384.0))
        rem = padded - hi * 16384.0
        mid = jnp.floor(rem * (1.0 / 128.0))
        low = rem - mid * 128.0
        psum = lambda v: jnp.dot(lower, v.astype(BF16), preferred_element_type=F32)
        pend = psum(hi) * 16384.0 + psum(mid) * 128.0 + psum(low)
        pstart = pend - padded

        bstart = lax.broadcasted_iota(I32, (rr, n_blk_lanes), 1).astype(F32) * blk
        ended = csum(jnp.where(pend[:, 0:1] <= bstart, 1.0, 0.0))
        blk_e = jnp.clip(ended - N_GROUPS, 0.0, N_EXPERTS - 1.0)
        n_used = pend[rr - 1:rr, 0:1] * (1.0 / blk)
        row8b = lax.broadcasted_iota(I32, (8, n_blk_lanes), 0)
        blk_ref[...] = jnp.where(row8b == 0, blk_e, jnp.where(row8b == 1, n_used, 0.0)).astype(I32)

        diag = lax.broadcasted_iota(I32, (rr, LANES), 0) == lax.broadcasted_iota(I32, (rr, LANES), 1)
        to_lanes = lambda v: csum(jnp.where(diag, v, 0.0))
        row8e = lax.broadcasted_iota(I32, (8, LANES), 0)
        exp_ref[...] = jnp.where(row8e == 0, to_lanes(carry),
                                 jnp.where(row8e == 1, to_lanes(pstart), 0.0)).astype(I32)

        ch = min(t, 2048)
        row_c = lax.broadcasted_iota(I32, (rr, ch), 0).astype(F32)
        row8c = lax.broadcasted_iota(I32, (8, ch), 0)

        def chunk(c, carry_):
            sl = pl.ds(pl.multiple_of(c * ch, ch), ch)
            cd = code_ref[:, sl]
            dest = lambda s: csum(jnp.where(row_c == cd[s:s + 1], pstart[:, 0:1], 0.0)) + cd[s + 2:s + 3]
            dest_ref[:, sl] = jnp.where(row8c == 0, dest(0), jnp.where(row8c == 1, dest(1), 0.0)).astype(I32)
            return carry_

        lax.fori_loop(0, t // ch, chunk, 0)


def _route(logits_t, tr, n_blocks):
    t = logits_t.shape[1]
    n_blk_lanes = -(-n_blocks // LANES) * LANES
    whole = lambda shape: pl.BlockSpec(shape, lambda i: (0, 0))
    return pl.pallas_call(
        functools.partial(_route_kernel, tr=tr, t=t, n_blk_lanes=n_blk_lanes),
        grid=(t // tr,),
        in_specs=[pl.BlockSpec((ROUTE_ROWS, tr), lambda i: (0, i))],
        out_specs=[whole((8, t)), pl.BlockSpec((8, tr), lambda i: (0, i)), whole((8, n_blk_lanes)), whole((8, LANES))],
        out_shape=[jax.ShapeDtypeStruct((8, t), I32), jax.ShapeDtypeStruct((8, t), F32),
                   jax.ShapeDtypeStruct((8, n_blk_lanes), I32), jax.ShapeDtypeStruct((8, LANES), I32)],
        scratch_shapes=[pltpu.VMEM((8, t), F32), pltpu.VMEM((ROUTE_ROWS, LANES), F32)],
        compiler_params=_cparams(1),
        name="route",
    )(logits_t)


ISSUE_UNROLL = 8


def _slab_rows(ref, row, ns):
    return ref.at[pl.ds(pl.multiple_of(row * ns, ns), ns)]


def _dispatch_kernel(ps_ref, sz_ref, d1_ref, d2_ref, hn_ref, wg_ref, wu_ref, wd_ref,
                     xp_ref, wgb_ref, wub_ref, wdb_ref, zero_ref, sem, *, tr, ns):
    step = pl.program_id(0)
    base = step * tr

    def copy(i, d):
        return pltpu.make_async_copy(_slab_rows(hn_ref, i, ns), _slab_rows(xp_ref, d, ns), sem)

    def pad_copy(d):
        return pltpu.make_async_copy(zero_ref.at[pl.ds(0, ns)], _slab_rows(xp_ref, d, ns), sem)

    def pad_block_copy(blk):
        return pltpu.make_async_copy(zero_ref, _slab_rows(xp_ref, blk, MOE_BLOCK * ns), sem)

    def issue(g, carry):
        for u in range(ISSUE_UNROLL):
            i = g * ISSUE_UNROLL + u
            copy(i, d1_ref[base + i]).start(priority=0)
            copy(i, d2_ref[base + i]).start(priority=1)
        return carry

    def drain(g, carry):
        for _ in range(2 * ISSUE_UNROLL):
            copy(0, 0).wait()
        return carry

    lax.fori_loop(0, tr // ISSUE_UNROLL, issue, 0)
    wgb_ref[...] = wg_ref[...].astype(BF16)
    wub_ref[...] = wu_ref[...].astype(BF16)
    wdb_ref[...] = wd_ref[...].astype(BF16)
    lax.fori_loop(0, tr // ISSUE_UNROLL, drain, 0)

    @pl.when(step == pl.num_programs(0) - 1)
    def _():
        zero_ref[...] = jnp.zeros_like(zero_ref)

        def pad_expert(e, n_pad):
            n = sz_ref[e]
            first = ps_ref[e] + n
            n_e = (MOE_BLOCK - n % MOE_BLOCK) % MOE_BLOCK

            def one(r, c):
                pad_copy(first + r).start()
                return c

            lax.fori_loop(0, n_e, one, 0)
            return n_pad + n_e

        n_pad = lax.fori_loop(0, N_EXPERTS, pad_expert, 0)

        def drain_pad(r, c):
            pad_copy(0).wait()
            return c

        lax.fori_loop(0, n_pad, drain_pad, 0)

        first_blk = (ps_ref[N_EXPERTS - 1] + sz_ref[N_EXPERTS - 1] + MOE_BLOCK - 1) // MOE_BLOCK
        n_blocks = xp_ref.shape[0] // (MOE_BLOCK * ns)

        def tail_start(blk, c):
            pad_block_copy(blk).start()
            return c

        def tail_wait(blk, c):
            pad_block_copy(blk).wait()
            return c

        lax.fori_loop(first_blk, n_blocks, tail_start, 0)
        lax.fori_loop(first_blk, n_blocks, tail_wait, 0)


def _dispatch(pstart, sizes, dest1, dest2, hn3_slab, wg, wu, wd, n_rows, ns):
    t = dest1.shape[0]
    n_e, d, ff = wg.shape
    tr = t // n_e
    assert tr * n_e == t and tr % ISSUE_UNROLL == 0
    per_expert = lambda r, c: pl.BlockSpec((None, r, c), lambda i, *_: (i, 0, 0))
    return pl.pallas_call(
        functools.partial(_dispatch_kernel, tr=tr, ns=ns),
        grid_spec=pltpu.PrefetchScalarGridSpec(
            num_scalar_prefetch=4,
            grid=(n_e,),
            in_specs=[pl.BlockSpec((tr * ns, LANES), lambda i, *_: (i, 0)),
                      per_expert(d, ff), per_expert(d, ff), per_expert(ff, d)],
            out_specs=[pl.BlockSpec(memory_space=pl.ANY),
                       per_expert(d, ff), per_expert(d, ff), per_expert(ff, d)],
            scratch_shapes=[pltpu.VMEM((MOE_BLOCK * ns, LANES), U32), pltpu.SemaphoreType.DMA],
        ),
        out_shape=[jax.ShapeDtypeStruct((n_rows * ns, LANES), U32),
                   jax.ShapeDtypeStruct(wg.shape, BF16), jax.ShapeDtypeStruct(wu.shape, BF16),
                   jax.ShapeDtypeStruct(wd.shape, BF16)],
        compiler_params=_cparams(1),
        name="dispatch",
    )(pstart, sizes, dest1, dest2, hn3_slab, wg, wu, wd)


BLOCKS_PER_STEP = 4


def _expert_kernel(be_ref, nu_ref, x_ref, *refs):
    del be_ref
    w_refs, y_ref = refs[:-1], refs[-1]
    rows = y_ref.shape[0] // BLOCKS_PER_STEP
    n_live = nu_ref[0] - pl.program_id(0) * BLOCKS_PER_STEP

    def mlp(j):
        wg_ref, wu_ref, wd_ref = w_refs[3 * j:3 * j + 3]
        x = _from_slab(x_ref.at[pl.ds(j * rows, rows)], MOE_BLOCK).astype(BF16)
        hmid = _silu(_dot(x, wg_ref[...])) * _dot(x, wu_ref[...])
        return _dot(hmid, wd_ref[...])

    for live in range(BLOCKS_PER_STEP + 1):
        if live == 0:
            cond = n_live <= 0
        elif live == BLOCKS_PER_STEP:
            cond = n_live >= live
        else:
            cond = n_live == live

        @pl.when(cond)
        def _(live=live):
            ys = [mlp(j) for j in range(live)]
            for j in range(BLOCKS_PER_STEP):
                out = y_ref.at[pl.ds(j * rows, rows)]
                if j < live:
                    _to_slab(out, ys[j])
                else:
                    out[...] = jnp.zeros_like(out)


def _experts(blk_e, n_used, x_pad, wg, wu, wd):
    d, ff = wg.shape[-2:]
    blk = MOE_BLOCK * _slab_rows_per_token(d)
    n_steps = x_pad.shape[0] // (blk * BLOCKS_PER_STEP)
    assert n_steps * blk * BLOCKS_PER_STEP == x_pad.shape[0]
    rows = lambda i, be, nu: (jnp.minimum(i, (nu[0] - 1) // BLOCKS_PER_STEP), 0)
    weights = lambda j, r, c: pl.BlockSpec((None, r, c), lambda i, be, nu: (be[i * BLOCKS_PER_STEP + j], 0, 0))
    w_specs, w_args = [], []
    for j in range(BLOCKS_PER_STEP):
        w_specs += [weights(j, d, ff), weights(j, d, ff), weights(j, ff, d)]
        w_args += [wg, wu, wd]
    return pl.pallas_call(
        _expert_kernel,
        grid_spec=pltpu.PrefetchScalarGridSpec(
            num_scalar_prefetch=2,
            grid=(n_steps,),
            in_specs=[pl.BlockSpec((blk * BLOCKS_PER_STEP, LANES), rows)] + w_specs,
            out_specs=pl.BlockSpec((blk * BLOCKS_PER_STEP, LANES), lambda i, be, nu: (i, 0)),
        ),
        out_shape=jax.ShapeDtypeStruct(x_pad.shape, U32),
        compiler_params=_cparams(1),
        name="experts",
    )(blk_e, n_used, x_pad, *w_args)


def _combine_kernel(d1_ref, d2_ref, h2_ref, wt_ref, fnw_ref, y_ref, out_ref, b1_ref, b2_ref, sem, *, tr, ns):
    step = pl.program_id(0)
    n_steps = pl.num_programs(0)

    def copy(d, buf, slot, i):
        return pltpu.make_async_copy(_slab_rows(y_ref, d, ns), _slab_rows(buf.at[slot], i, ns), sem.at[slot])

    def issue_step(st):
        slot = st % 2

        def issue(g, carry):
            for u in range(ISSUE_UNROLL):
                i = g * ISSUE_UNROLL + u
                copy(d1_ref[st * tr + i], b1_ref, slot, i).start(priority=0)
                copy(d2_ref[st * tr + i], b2_ref, slot, i).start(priority=1)
            return carry

        lax.fori_loop(0, tr // ISSUE_UNROLL, issue, 0)

    @pl.when(step == 0)
    def _():
        issue_step(step)

    @pl.when(step + 1 < n_steps)
    def _():
        issue_step(step + 1)

    slot = step % 2

    def drain(g, carry):
        for _ in range(ISSUE_UNROLL):
            copy(0, b1_ref, slot, 0).wait()
            copy(0, b2_ref, slot, 0).wait()
        return carry

    lax.fori_loop(0, tr // ISSUE_UNROLL, drain, 0)

    wt = wt_ref[...].T
    h3 = (h2_ref[...] + wt[:, 0:1] * _from_slab(b1_ref.at[slot], tr)
          + wt[:, 1:2] * _from_slab(b2_ref.at[slot], tr))
    out_ref[...] = _rms(h3) * fnw_ref[...]


def _combine(dest1, dest2, h2, wts, fnw, y_pad, tr):
    t, d = h2.shape
    ns = _slab_rows_per_token(d)
    return pl.pallas_call(
        functools.partial(_combine_kernel, tr=tr, ns=ns),
        grid_spec=pltpu.PrefetchScalarGridSpec(
            num_scalar_prefetch=2,
            grid=(t // tr,),
            in_specs=[pl.BlockSpec((tr, d), lambda i, *_: (i, 0)),
                      pl.BlockSpec((8, tr), lambda i, *_: (0, i)),
                      pl.BlockSpec((1, d), lambda i, *_: (0, 0)),
                      pl.BlockSpec(memory_space=pl.ANY)],
            out_specs=pl.BlockSpec((tr, d), lambda i, *_: (i, 0)),
            scratch_shapes=[pltpu.VMEM((2, tr * ns, LANES), U32), pltpu.VMEM((2, tr * ns, LANES), U32),
                            pltpu.SemaphoreType.DMA((2,))],
        ),
        out_shape=jax.ShapeDtypeStruct((t, d), F32),
        compiler_params=_cparams(1),
        name="combine",
    )(dest1, dest2, h2, wts, fnw, y_pad)


def _pick(n, pref):
    while n % pref:
        pref //= 2
    return pref


def kernel(x, mem, norm_mix_w, w_in, conv_w, gdn_a_log, gdn_dt_bias, gdn_out_norm_w, hgrn_lb, hgrn_out_norm_w, w_branch_a, w_branch_b, w_out, norm_xattn_w, norm_mem_w, xattn_wq, xattn_wkv, xattn_wo, norm_ffn_w, router_group_w, router_group_b, router_expert_w, router_expert_b, expert_w_gate, expert_w_up, expert_w_down, final_norm_w):
    b, s, d = x.shape
    t = b * s
    depth = w_in.shape[0]
    w = HEADS * HEAD_DIM
    qkv_w = 3 * w
    assert d == w and s % CHUNK == 0

    tc = _pick(s, 256)
    tm_proj = _pick(t, 1024)
    tm_post = _pick(s, 512)
    tr = _pick(t, 256)

    def pad_lanes(v, offset=0):
        return jnp.zeros((1, LANES), F32).at[0, offset:offset + v.shape[0]].set(v.astype(F32))

    assert depth == 1
    h3d = x
    for layer in range(depth):
        wl = w_in[layer]
        w_main = jnp.concatenate([wl[:, :qkv_w], wl[:, qkv_w + 2 * HEADS:]], axis=1).astype(BF16)
        p_main, p_small = _proj(h3d.reshape(t, d), norm_mix_w[layer][None, :], w_main, wl, qkv_w, 2 * HEADS,
                                tm_proj, 2048)
        p3 = p_main.reshape(b, s, -1)
        o_a = _gdn(p3, p_small.reshape(b, s, LANES), conv_w[layer], pad_lanes(gdn_a_log[layer]),
                   pad_lanes(gdn_dt_bias[layer]), tc)
        o_b = _hgrn(p3, hgrn_lb[layer:], tc)
        kv = _kv(mem, norm_mem_w[layer][None, :], xattn_wkv[layer].astype(BF16))
        w_router = jnp.zeros((d, LANES), F32)
        w_router = w_router.at[:, :N_GROUPS].set(router_group_w[layer])
        w_router = w_router.at[:, N_GROUPS:N_GROUPS + N_EXPERTS].set(router_expert_w[layer])
        b_router = pad_lanes(router_group_b[layer]) + pad_lanes(router_expert_b[layer], N_GROUPS)
        tile8 = lambda v: jnp.tile(v.astype(F32), HEADS)[None, :]
        h2, hn3, logits_t = _post(
            h3d, o_a, o_b, p3, kv, tile8(gdn_out_norm_w[layer]), tile8(hgrn_out_norm_w[layer]),
            w_branch_a[layer].astype(BF16), w_branch_b[layer].astype(BF16), w_out[layer].astype(BF16),
            norm_xattn_w[layer][None, :], xattn_wq[layer].astype(BF16), xattn_wo[layer].astype(BF16),
            norm_ffn_w[layer][None, :], w_router, b_router, tm_post)

        m = t * 2
        n_rows = ((m + MOE_BLOCK - 1) // MOE_BLOCK) * MOE_BLOCK + N_EXPERTS * MOE_BLOCK
        n_blocks = -(-(n_rows // MOE_BLOCK) // BLOCKS_PER_STEP) * BLOCKS_PER_STEP
        n_rows = n_blocks * MOE_BLOCK
        dest, wts, blk_tab, exp_tab = _route(logits_t, _pick(t, 1024), n_blocks)
        dest1, dest2 = dest[0], dest[1]
        blk_e, n_used = blk_tab[0, :n_blocks], blk_tab[1, :1]
        sizes = exp_tab[0, N_GROUPS:N_GROUPS + N_EXPERTS]
        pstart = exp_tab[1, N_GROUPS:N_GROUPS + N_EXPERTS]

        x_pad, wg, wu, wd = _dispatch(pstart, sizes, dest1, dest2, hn3, expert_w_gate[layer], expert_w_up[layer],
                                      expert_w_down[layer], n_rows, _slab_rows_per_token(d))
        y_pad = _experts(blk_e, n_used, x_pad, wg, wu, wd)
        out = _combine(dest1, dest2, h2.reshape(t, d), wts, final_norm_w[None, :], y_pad, tr)
        h3d = out.reshape(b, s, d)
    return h3d
```

```python
import functools

import jax
import jax.numpy as jnp
from jax import lax
from jax.experimental import pallas as pl
from jax.experimental.pallas import tpu as pltpu

F32 = jnp.float32
BF16 = jnp.bfloat16
I32 = jnp.int32
U32 = jnp.uint32

EPS = 1e-6
CHUNK = 64
HEADS = 8
HEAD_DIM = 128
CONV_K = 4
XA_HEADS = 4
N_GROUPS = 4
EXP_PER_GROUP = 8
N_EXPERTS = N_GROUPS * EXP_PER_GROUP
MOE_BLOCK = 256
LANES = 128

VMEM_LIMIT = 52 * 1024 * 1024


def _cparams(n_axes):
    return pltpu.CompilerParams(dimension_semantics=("arbitrary",) * n_axes,
                                vmem_limit_bytes=VMEM_LIMIT)


def _dot(a, b):
    return jnp.dot(a.astype(BF16), b.astype(BF16), preferred_element_type=F32)


def _dot_nt(a, b):
    return lax.dot_general(a.astype(BF16), b.astype(BF16), (((1,), (1,)), ((), ())),
                           preferred_element_type=F32)


def _dot_tn(a, b):
    return lax.dot_general(a.astype(BF16), b.astype(BF16), (((0,), (0,)), ((), ())),
                           preferred_element_type=F32)


def _split(x):
    hi = x.astype(BF16)
    lo = (x - hi.astype(F32)).astype(BF16)
    return hi, lo


def _dot_exact_lhs(m_bf16, x):
    hi, lo = _split(x)
    return (jnp.dot(m_bf16, hi, preferred_element_type=F32)
            + jnp.dot(m_bf16, lo, preferred_element_type=F32))


def _rms(x):
    return x * lax.rsqrt(jnp.mean(x * x, axis=-1, keepdims=True) + EPS)


def _silu(x):
    return x * jax.nn.sigmoid(x)


def _softplus(x):
    return jnp.maximum(x, 0.0) + jnp.log(1.0 + jnp.exp(-jnp.abs(x)))


HIGH_HALF = 0xFFFF0000


def _slab_rows_per_token(d):
    return d // (2 * LANES)


def _to_slab(ref, x):
    n, d = x.shape
    ns = _slab_rows_per_token(d)
    bits = lambda v: pltpu.bitcast(v.astype(BF16).astype(F32), U32)
    for s in range(ns):
        lo = bits(x[:, s * LANES:(s + 1) * LANES])
        hi = bits(x[:, (s + ns) * LANES:(s + ns + 1) * LANES])
        ref[pl.ds(s, n, stride=ns), :] = (lo >> 16) | (hi & jnp.uint32(HIGH_HALF))


def _from_slab(ref, n):
    ns = ref.shape[0] // n
    words = [ref[pl.ds(s, n, stride=ns), :] for s in range(ns)]
    lo = [pltpu.bitcast(wd << 16, F32) for wd in words]
    hi = [pltpu.bitcast(wd & jnp.uint32(HIGH_HALF), F32) for wd in words]
    return jnp.concatenate(lo + hi, axis=1)


def _chunk_masks(tc):
    ri = lax.broadcasted_iota(I32, (tc, tc), 0)
    ci = lax.broadcasted_iota(I32, (tc, tc), 1)
    same = (ri // CHUNK) == (ci // CHUNK)
    causal = same & (ri >= ci)
    strict = same & (ri > ci)
    return ri, ci, causal, strict


def _wprep_kernel(wt_ref, main_ref, small_ref, tail_ref, *, skip_blk, n_small):
    i = pl.program_id(0)
    last = pl.num_programs(0) - 1
    rows, d = wt_ref.shape
    keep = rows - n_small

    @pl.when(i < skip_blk)
    def _():
        main_ref[...] = wt_ref[...].astype(BF16)

    @pl.when(i == skip_blk)
    def _():
        small_ref[...] = jnp.concatenate([wt_ref[0:n_small, :].astype(BF16),
                                          jnp.zeros((LANES - n_small, d), BF16)], axis=0)

    @pl.when(i > skip_blk)
    def _():
        main_ref[0:keep, :] = tail_ref[0:keep, :]
        main_ref[keep:rows, :] = wt_ref[0:n_small, :].astype(BF16)

    @pl.when((i >= skip_blk) & (i < last))
    def _():
        tail_ref[0:keep, :] = wt_ref[n_small:rows, :].astype(BF16)


def _wprep(w_t, skip_row, n_small, rows):
    n_in, d = w_t.shape
    assert skip_row % rows == 0 and (n_in - n_small) % rows == 0 and n_small % 16 == 0
    skip_blk = skip_row // rows
    n_out_blk = (n_in - n_small) // rows
    out_blk = lambda i: (jnp.where(i <= skip_blk, jnp.minimum(i, skip_blk - 1), i - 1), 0)
    return pl.pallas_call(
        functools.partial(_wprep_kernel, skip_blk=skip_blk, n_small=n_small),
        grid=(n_out_blk + 1,),
        in_specs=[pl.BlockSpec((rows, d), lambda i: (i, 0))],
        out_specs=[pl.BlockSpec((rows, d), out_blk), pl.BlockSpec((LANES, d), lambda i: (0, 0))],
        out_shape=[jax.ShapeDtypeStruct((n_in - n_small, d), BF16), jax.ShapeDtypeStruct((LANES, d), BF16)],
        scratch_shapes=[pltpu.VMEM((rows, d), BF16)],
        compiler_params=_cparams(1),
        name="wprep",
    )(w_t)


def _proj_kernel(x_ref, nw_ref, w_ref, ws_ref, out_ref, small_ref, hn_ref):
    nt = (((1,), (1,)), ((), ()))

    @pl.when(pl.program_id(1) == 0)
    def _():
        hn = (_rms(x_ref[...]) * nw_ref[...]).astype(BF16)
        hn_ref[...] = hn
        small_ref[...] = lax.dot_general(hn, ws_ref[...], nt, preferred_element_type=F32)

    out_ref[...] = lax.dot_general(hn_ref[...], w_ref[...], nt, preferred_element_type=F32).astype(BF16)


def _proj(x2, nw, w_main_t, w_small_t, tm, tn):
    t, d = x2.shape
    n = w_main_t.shape[0]
    return pl.pallas_call(
        _proj_kernel,
        grid=(t // tm, n // tn),
        in_specs=[
            pl.BlockSpec((tm, d), lambda i, j: (i, 0)),
            pl.BlockSpec((1, d), lambda i, j: (0, 0)),
            pl.BlockSpec((tn, d), lambda i, j: (j, 0)),
            pl.BlockSpec((LANES, d), lambda i, j: (0, 0)),
        ],
        out_specs=[
            pl.BlockSpec((tm, tn), lambda i, j: (i, j)),
            pl.BlockSpec((tm, LANES), lambda i, j: (i, 0)),
        ],
        out_shape=[jax.ShapeDtypeStruct((t, n), BF16), jax.ShapeDtypeStruct((t, LANES), F32)],
        scratch_shapes=[pltpu.VMEM((tm, d), BF16)],
        compiler_params=_cparams(2),
        name="proj",
    )(x2, nw, w_main_t, w_small_t)


def _gdn_kernel(q_ref, k_ref, v_ref, sm_ref, cw_ref, alog_ref, dtb_ref, o_ref, xs_ref, st_ref, *, tc):
    w = HEADS * HEAD_DIM
    nc = tc // CHUNK

    @pl.when(pl.program_id(1) == 0)
    def _():
        xs_ref[0:8, :] = jnp.zeros((8, 3 * w), F32)
        st_ref[...] = jnp.zeros_like(st_ref)

    _, _, causal, strict = _chunk_masks(tc)
    tri = jnp.where(causal, 1.0, 0.0).astype(BF16)
    wi = lax.broadcasted_iota(I32, (CHUNK, tc), 0)
    wj = lax.broadcasted_iota(I32, (CHUNK, tc), 1)
    eye_w = jnp.where(wi == wj % CHUNK, 1.0, 0.0)
    blk_w = wj // CHUNK

    def fold(m_bd):
        acc = m_bd[0:CHUNK]
        for c in range(1, nc):
            acc = acc + m_bd[c * CHUNK:(c + 1) * CHUNK]
        return acc

    def unfold(m_w):
        return jnp.concatenate([jnp.where(blk_w == c, m_w, 0.0) for c in range(nc)], axis=0)

    sm = sm_ref[...]
    lane = lax.broadcasted_iota(I32, (tc, LANES), 1)
    g_all = jnp.where(lane < HEADS, -jnp.exp(alog_ref[...]) * _softplus(sm + dtb_ref[...]), 0.0)
    beta_all = jax.nn.sigmoid(sm)
    cum = _dot_exact_lhs(tri, g_all)
    ecum = jnp.exp(cum)
    cum_t = cum.T

    sr = lax.broadcasted_iota(I32, ((CONV_K - 1) * tc, tc), 0)
    sc = lax.broadcasted_iota(I32, ((CONV_K - 1) * tc, tc), 1)
    assert tc & (tc - 1) == 0
    shifts = jnp.where(sc == (sr & (tc - 1)) - ((sr >> (tc.bit_length() - 1)) + 1), 1.0, 0.0).astype(BF16)

    def conv_part(p, ref):
        cols = slice(p * w, (p + 1) * w)
        xb = ref[...]
        shifted = jnp.dot(shifts, xb, preferred_element_type=F32)
        acc = cw_ref[CONV_K - 1:CONV_K, cols] * xb.astype(F32)
        for s in range(1, CONV_K):
            acc = acc + cw_ref[CONV_K - 1 - s:CONV_K - s, cols] * shifted[(s - 1) * tc:s * tc]
        xs_ref[8:16, cols] = xb[0:8].astype(F32)
        first = cw_ref[CONV_K - 1:CONV_K, cols] * xs_ref[8:16, cols]
        for s in range(1, CONV_K):
            first = first + cw_ref[CONV_K - 1 - s:CONV_K - s, cols] * xs_ref[8 - s:16 - s, cols]
        xs_ref[0:8, cols] = xb[tc - 8:tc].astype(F32)
        return _silu(jnp.concatenate([first, acc[8:]], axis=0))

    qkv = [conv_part(p, ref) for p, ref in enumerate((q_ref, k_ref, v_ref))]

    def l2n(x):
        return x * lax.rsqrt(jnp.sum(x * x, axis=-1, keepdims=True) + EPS)

    heads = range(HEADS)
    ks, a_bd, qk_bd, rhs, qd, cc = [], [], [], [], [], []
    for h in heads:
        hc = slice(h * HEAD_DIM, (h + 1) * HEAD_DIM)
        q = l2n(qkv[0][:, hc]) * HEAD_DIM ** -0.5
        k = l2n(qkv[1][:, hc])
        v = qkv[2][:, hc]
        cch = cum[:, h:h + 1]
        cr = cum_t[h:h + 1, :]
        bc = beta_all[:, HEADS + h:HEADS + h + 1]
        ec = ecum[:, h:h + 1]
        dec = jnp.where(causal, jnp.exp(jnp.where(causal, cch - cr, 0.0)), 0.0)
        kb = k * bc
        a_bd.append(jnp.where(strict, _dot_nt(kb, k) * dec, 0.0))
        qk_bd.append(_dot_nt(q, k) * dec)
        rhs.append(jnp.concatenate([v * bc, kb * ec], axis=1).astype(BF16))
        qd.append(q * ec)
        ks.append(k)
        cc.append(cch)

    xw, pw = [], []
    for h in heads:
        aw = fold(a_bd[h])
        xw.append(eye_w - aw)
        pw.append(_dot(aw, a_bd[h]))
    n_sq = CHUNK.bit_length() - 2
    for it in range(n_sq):
        for h in heads:
            p_bd = unfold(pw[h]).astype(BF16)
            if it + 1 < n_sq:
                res = _dot(jnp.concatenate([xw[h], pw[h]], axis=0), p_bd)
                xw[h] = xw[h] + res[:CHUNK]
                pw[h] = res[CHUNK:]
            else:
                xw[h] = xw[h] + _dot(xw[h], p_bd)

    uw, qp, o0 = [], [], []
    for h in heads:
        uwh = _dot(unfold(xw[h]), rhs[h])
        qkuw = _dot(qk_bd[h], uwh)
        uw.append(uwh.astype(BF16))
        o0.append(qkuw[:, :HEAD_DIM])
        qp.append(qd[h] - qkuw[:, HEAD_DIM:])

    state = [st_ref[h] for h in heads]
    for c in range(nc):
        r0 = c * CHUNK
        rows = slice(r0, r0 + CHUNK)
        kuw, dlast = [], []
        for h in heads:
            last = cc[h][r0 + CHUNK - 1:r0 + CHUNK, :]
            kd = ks[h][rows] * jnp.exp(last - cc[h][rows])
            kuw.append(_dot_tn(kd, uw[h][rows]))
            dlast.append(jnp.exp(last))
        for h in heads:
            s = state[h]
            res = _dot(jnp.concatenate([kuw[h][:, HEAD_DIM:], qp[h][rows]], axis=0), s)
            o = res[HEAD_DIM:] + o0[h][rows]
            state[h] = s * dlast[h] - res[:HEAD_DIM] + kuw[h][:, :HEAD_DIM]
            o_ref[rows, h * HEAD_DIM:(h + 1) * HEAD_DIM] = _rms(o).astype(BF16)
    for h in heads:
        st_ref[h] = state[h]


def _gdn(p3, small3, conv_w, alog, dtb, tc):
    b, s, _ = p3.shape
    w = HEADS * HEAD_DIM
    return pl.pallas_call(
        functools.partial(_gdn_kernel, tc=tc),
        grid=(b, s // tc),
        in_specs=[
            pl.BlockSpec((None, tc, w), lambda i, j: (i, j, 0)),
            pl.BlockSpec((None, tc, w), lambda i, j: (i, j, 1)),
            pl.BlockSpec((None, tc, w), lambda i, j: (i, j, 2)),
            pl.BlockSpec((None, tc, LANES), lambda i, j: (i, j, 0)),
            pl.BlockSpec((CONV_K, 3 * w), lambda i, j: (0, 0)),
            pl.BlockSpec((1, LANES), lambda i, j: (0, 0)),
            pl.BlockSpec((1, LANES), lambda i, j: (0, 0)),
        ],
        out_specs=pl.BlockSpec((None, tc, w), lambda i, j: (i, j, 0)),
        out_shape=jax.ShapeDtypeStruct((b, s, w), BF16),
        scratch_shapes=[pltpu.VMEM((16, 3 * w), F32), pltpu.VMEM((HEADS, HEAD_DIM, HEAD_DIM), F32)],
        compiler_params=_cparams(2),
        name="gdn",
    )(p3, p3, p3, small3, conv_w, alog, dtb)


def _hgrn_kernel(f_ref, q_ref, i_ref, lb_ref, o_ref, st_ref, *, tc):
    nc = tc // CHUNK

    @pl.when(pl.program_id(1) == 0)
    def _():
        st_ref[...] = jnp.zeros_like(st_ref)

    _, _, causal, _ = _chunk_masks(tc)
    tri = jnp.where(causal, 1.0, 0.0).astype(BF16)

    lbp = lb_ref[...]
    lbe = jnp.exp(lbp - jnp.max(lbp, axis=0, keepdims=True))
    lb = lbe[0:1, :] / jnp.sum(lbe, axis=0, keepdims=True)

    heads = range(HEADS)
    hcols = [slice(h * HEAD_DIM, (h + 1) * HEAD_DIM) for h in heads]
    forget = lb + (1.0 - lb) * jax.nn.sigmoid(f_ref[...].astype(F32))
    cum = _dot_exact_lhs(tri, jnp.log(forget))
    kk = 1.0 - forget
    q_in = (q_ref[...].astype(F32) * HEAD_DIM ** -0.5 * jnp.exp(cum)).astype(BF16)
    k_in = (kk * jnp.exp(-cum)).astype(BF16)
    v = i_ref[...]
    intra = [jnp.where(causal, _dot_nt(q_in[:, hc], k_in[:, hc]), 0.0).astype(BF16) for hc in hcols]
    state = [st_ref[h] for h in heads]
    for c in range(nc):
        r0 = c * CHUNK
        rows = slice(r0, r0 + CHUNK)
        last = cum[r0 + CHUNK - 1:r0 + CHUNK, :]
        k_dec = (kk[rows] * jnp.exp(last - cum[rows])).astype(BF16)
        dlast = jnp.exp(last)
        for h, hc in zip(heads, hcols):
            o = _dot_nt(q_in[rows, hc], state[h]) + _dot(intra[h][rows, rows], v[rows, hc])
            state[h] = state[h] * dlast[:, hc] + _dot_tn(v[rows, hc], k_dec[:, hc])
            o_ref[rows, hc] = _rms(o).astype(BF16)
    for h in heads:
        st_ref[h] = state[h]


def _hgrn(p3, lb_logits, tc):
    b, s, _ = p3.shape
    w = HEADS * HEAD_DIM
    return pl.pallas_call(
        functools.partial(_hgrn_kernel, tc=tc),
        grid=(b, s // tc),
        in_specs=[
            pl.BlockSpec((None, tc, w), lambda i, j: (i, j, 4)),
            pl.BlockSpec((None, tc, w), lambda i, j: (i, j, 5)),
            pl.BlockSpec((None, tc, w), lambda i, j: (i, j, 6)),
            pl.BlockSpec(lb_logits.shape, lambda i, j: (0, 0)),
        ],
        out_specs=pl.BlockSpec((None, tc, w), lambda i, j: (i, j, 0)),
        out_shape=jax.ShapeDtypeStruct((b, s, w), BF16),
        scratch_shapes=[pltpu.VMEM((HEADS, HEAD_DIM, HEAD_DIM), F32)],
        compiler_params=_cparams(2),
        name="hgrn",
    )(p3, p3, p3, lb_logits)


def _kv_kernel(mem_ref, nw_ref, wkv_ref, kv_ref):
    mn = (_rms(mem_ref[...]) * nw_ref[...]).astype(BF16)
    kv_ref[...] = jnp.dot(mn, wkv_ref[...], preferred_element_type=F32).astype(BF16)


def _kv(mem, nw, wkv):
    b, m, d = mem.shape
    return pl.pallas_call(
        _kv_kernel,
        grid=(b,),
        in_specs=[
            pl.BlockSpec((None, m, d), lambda i: (i, 0, 0)),
            pl.BlockSpec((1, d), lambda i: (0, 0)),
            pl.BlockSpec(wkv.shape, lambda i: (0, 0)),
        ],
        out_specs=pl.BlockSpec((None, m, 2 * d), lambda i: (i, 0, 0)),
        out_shape=jax.ShapeDtypeStruct((b, m, 2 * d), BF16),
        compiler_params=_cparams(1),
        name="kv",
    )(mem, nw, wkv)


def _post_kernel(x_ref, oa_ref, ob_ref, oga_ref, ogb_ref, ga_ref, gb_ref, kv_ref,
                 gnw_ref, hnw_ref, wa_ref, wb_ref, wout_ref, nx_ref, wq_ref, wo_ref, nf_ref,
                 wr_ref, br_ref, h2_ref, hn3_ref, lg_ref):
    d = x_ref.shape[-1]
    dh = d // XA_HEADS
    ya = oa_ref[...].astype(F32) * gnw_ref[...] * _silu(oga_ref[...].astype(F32))
    yb = ob_ref[...].astype(F32) * hnw_ref[...] * _silu(ogb_ref[...].astype(F32))
    merged = (jax.nn.sigmoid(ga_ref[...].astype(F32)) * _dot(ya, wa_ref[...])
              + jax.nn.sigmoid(gb_ref[...].astype(F32)) * _dot(yb, wb_ref[...]))
    h1 = x_ref[...] + _dot(merged, wout_ref[...])

    q = _dot(_rms(h1) * nx_ref[...], wq_ref[...]) * dh ** -0.5
    outs = []
    for hh in range(XA_HEADS):
        kh = kv_ref[:, hh * dh:(hh + 1) * dh]
        vh = kv_ref[:, d + hh * dh:d + (hh + 1) * dh]
        sc = _dot_nt(q[:, hh * dh:(hh + 1) * dh], kh)
        p = jnp.exp(sc - jnp.max(sc, axis=-1, keepdims=True))
        outs.append(_dot(p, vh) / jnp.sum(p, axis=-1, keepdims=True))
    h2 = h1 + _dot(jnp.concatenate(outs, axis=1), wo_ref[...])
    h2_ref[...] = h2

    hn3 = _rms(h2) * nf_ref[...]
    _to_slab(hn3_ref, hn3)
    hi, lo = _split(hn3)
    whi, wlo = _split(wr_ref[...])
    hw = jnp.dot(hi, jnp.concatenate([whi, wlo], axis=1), preferred_element_type=F32)
    lg = (hw[:, :LANES] + hw[:, LANES:] + jnp.dot(lo, whi, preferred_element_type=F32)) + br_ref[...]
    lg_ref[...] = lg.T[0:lg_ref.shape[0], :]


def _post(x3, oa, ob, p3, kv, gnw, hnw, wa, wb, wout, nx, wq, wo, nf, wr, br, tm):
    b, s, d = x3.shape
    ns = _slab_rows_per_token(d)
    row = lambda c: pl.BlockSpec((None, tm, d), lambda i, j: (i, j, c))
    full = lambda a: pl.BlockSpec(a.shape, lambda i, j: (0,) * a.ndim, pipeline_mode=pl.Buffered(1))
    return pl.pallas_call(
        _post_kernel,
        grid=(b, s // tm),
        in_specs=[row(0), row(0), row(0), row(3), row(7), row(8), row(9),
                  pl.BlockSpec((None,) + kv.shape[1:], lambda i, j: (i, 0, 0)),
                  full(gnw), full(hnw), full(wa), full(wb), full(wout), full(nx), full(wq), full(wo),
                  full(nf), full(wr), full(br)],
        out_specs=[row(0), pl.BlockSpec((tm * ns, LANES), lambda i, j: (i * (s // tm) + j, 0)),
                   pl.BlockSpec((ROUTE_ROWS, tm), lambda i, j: (0, i * (s // tm) + j))],
        out_shape=[jax.ShapeDtypeStruct((b, s, d), F32), jax.ShapeDtypeStruct((b * s * ns, LANES), U32),
                   jax.ShapeDtypeStruct((ROUTE_ROWS, b * s), F32)],
        compiler_params=_cparams(2),
        name="post",
    )(x3, oa, ob, p3, p3, p3, p3, kv, gnw, hnw, wa, wb, wout, nx, wq, wo, nf, wr, br)


ROUTE_ROWS = 40


def _route_kernel(lg_ref, dest_ref, wt_ref, blk_ref, exp_ref, code_ref, carry_ref, *, tr, t, n_blk_lanes):
    step = pl.program_id(0)
    rr = ROUTE_ROWS

    @pl.when(step == 0)
    def _():
        carry_ref[...] = jnp.zeros_like(carry_ref)

    neg, big = -1e30, 1e9
    lt = lg_ref[...]
    row_f = lax.broadcasted_iota(I32, (rr, tr), 0).astype(F32)
    cmax = lambda v: jnp.max(v, axis=0, keepdims=True)
    csum = lambda v: jnp.sum(v, axis=0, keepdims=True)
    first = lambda m: jnp.min(jnp.where(m, row_f, big), axis=0, keepdims=True)

    is_g = row_f < N_GROUPS
    gl = jnp.where(is_g, lt, neg)
    gmax = cmax(gl)
    gidx = first(gl == gmax)
    g_p = 1.0 / csum(jnp.where(is_g, jnp.exp(gl - gmax), 0.0))

    lo = N_GROUPS + gidx * EXP_PER_GROUP
    in_grp = (row_f >= lo) & (row_f < lo + EXP_PER_GROUP)
    el = jnp.where(in_grp, lt, neg)
    m1 = cmax(el)
    i1 = first(el == m1)
    el2 = jnp.where(row_f == i1, neg, el)
    m2 = cmax(el2)
    i2 = first(el2 == m2)
    esum = csum(jnp.where(in_grp, jnp.exp(el - m1), 0.0))
    p1 = 1.0 / esum
    p2 = jnp.exp(m2 - m1) / esum
    w1 = g_p * p1 / (p1 + p2)
    w2 = g_p * p2 / (p1 + p2)

    oh1 = row_f == i1
    oh2 = row_f == i2
    both = jnp.where(oh1 | oh2, 1.0, 0.0)
    sub = min(tr, MOE_BLOCK)
    ti = lax.broadcasted_iota(I32, (sub, sub), 0)
    tj = lax.broadcasted_iota(I32, (sub, sub), 1)
    earlier = jnp.where(ti < tj, 1.0, 0.0).astype(BF16)
    carry = carry_ref[...]
    r1, r2 = [], []
    for sb in range(tr // sub):
        cols = slice(sb * sub, (sb + 1) * sub)
        before = jnp.dot(both[:, cols].astype(BF16), earlier, preferred_element_type=F32) + carry[:, 0:1]
        r1.append(csum(jnp.where(oh1[:, cols], before, 0.0)))
        r2.append(csum(jnp.where(oh2[:, cols], before, 0.0)))
        carry = carry + jnp.sum(both[:, cols], axis=1, keepdims=True)
    carry_ref[...] = carry
    r1 = jnp.concatenate(r1, axis=1)
    r2 = jnp.concatenate(r2, axis=1)

    row8 = lax.broadcasted_iota(I32, (8, tr), 0)
    rows4 = lambda a, b, c, d: jnp.where(row8 == 0, a, jnp.where(row8 == 1, b, jnp.where(row8 == 2, c,
                                                                                      jnp.where(row8 == 3, d, 0.0))))
    wt_ref[...] = rows4(w1, w2, 0.0, 0.0)
    code_ref[:, pl.ds(pl.multiple_of(step * tr, tr), tr)] = rows4(i1, i2, r1, r2)

    @pl.when(step == pl.num_programs(0) - 1)
    def _():
        blk = float(MOE_BLOCK)
        padded = jnp.floor((carry + (blk - 1.0)) * (1.0 / blk)) * blk
        ri = lax.broadcasted_iota(I32, (rr, rr), 0)
        rj = lax.broadcasted_iota(I32, (rr, rr), 1)
        lower = jnp.where(rj <= ri, 1.0, 0.0).astype(BF16)
        hi = jnp.floor(padded * (1.0 / 16384.0))
        rem = padded - hi * 16384.0
        mid = jnp.floor(rem * (1.0 / 128.0))
        low = rem - mid * 128.0
        psum = lambda v: jnp.dot(lower, v.astype(BF16), preferred_element_type=F32)
        pend = psum(hi) * 16384.0 + psum(mid) * 128.0 + psum(low)
        pstart = pend - padded

        bstart = lax.broadcasted_iota(I32, (rr, n_blk_lanes), 1).astype(F32) * blk
        ended = csum(jnp.where(pend[:, 0:1] <= bstart, 1.0, 0.0))
        blk_e = jnp.clip(ended - N_GROUPS, 0.0, N_EXPERTS - 1.0)
        n_used = pend[rr - 1:rr, 0:1] * (1.0 / blk)
        row8b = lax.broadcasted_iota(I32, (8, n_blk_lanes), 0)
        blk_ref[...] = jnp.where(row8b == 0, blk_e, jnp.where(row8b == 1, n_used, 0.0)).astype(I32)

        diag = lax.broadcasted_iota(I32, (rr, LANES), 0) == lax.broadcasted_iota(I32, (rr, LANES), 1)
        to_lanes = lambda v: csum(jnp.where(diag, v, 0.0))
        row8e = lax.broadcasted_iota(I32, (8, LANES), 0)
        exp_ref[...] = jnp.where(row8e == 0, to_lanes(carry),
                                 jnp.where(row8e == 1, to_lanes(pstart), 0.0)).astype(I32)

        ch = min(t, 2048)
        row_c = lax.broadcasted_iota(I32, (rr, ch), 0).astype(F32)
        row8c = lax.broadcasted_iota(I32, (8, ch), 0)

        def chunk(c, carry_):
            sl = pl.ds(pl.multiple_of(c * ch, ch), ch)
            cd = code_ref[:, sl]
            dest = lambda s: csum(jnp.where(row_c == cd[s:s + 1], pstart[:, 0:1], 0.0)) + cd[s + 2:s + 3]
            dest_ref[:, sl] = jnp.where(row8c == 0, dest(0), jnp.where(row8c == 1, dest(1), 0.0)).astype(I32)
            return carry_

        lax.fori_loop(0, t // ch, chunk, 0)


def _route(logits_t, tr, n_blocks):
    t = logits_t.shape[1]
    n_blk_lanes = -(-n_blocks // LANES) * LANES
    whole = lambda shape: pl.BlockSpec(shape, lambda i: (0, 0))
    return pl.pallas_call(
        functools.partial(_route_kernel, tr=tr, t=t, n_blk_lanes=n_blk_lanes),
        grid=(t // tr,),
        in_specs=[pl.BlockSpec((ROUTE_ROWS, tr), lambda i: (0, i))],
        out_specs=[whole((8, t)), pl.BlockSpec((8, tr), lambda i: (0, i)), whole((8, n_blk_lanes)), whole((8, LANES))],
        out_shape=[jax.ShapeDtypeStruct((8, t), I32), jax.ShapeDtypeStruct((8, t), F32),
                   jax.ShapeDtypeStruct((8, n_blk_lanes), I32), jax.ShapeDtypeStruct((8, LANES), I32)],
        scratch_shapes=[pltpu.VMEM((8, t), F32), pltpu.VMEM((ROUTE_ROWS, LANES), F32)],
        compiler_params=_cparams(1),
        name="route",
    )(logits_t)


ISSUE_UNROLL = 8


def _slab_rows(ref, row, ns):
    return ref.at[pl.ds(pl.multiple_of(row * ns, ns), ns)]


def _dispatch_kernel(ps_ref, sz_ref, d1_ref, d2_ref, hn_ref, wg_ref, wu_ref, wd_ref,
                     xp_ref, wgb_ref, wub_ref, wdb_ref, zero_ref, sem, *, tr, ns):
    step = pl.program_id(0)
    base = step * tr

    def copy(i, d):
        return pltpu.make_async_copy(_slab_rows(hn_ref, i, ns), _slab_rows(xp_ref, d, ns), sem)

    def pad_copy(d):
        return pltpu.make_async_copy(zero_ref.at[pl.ds(0, ns)], _slab_rows(xp_ref, d, ns), sem)

    def pad_block_copy(blk):
        return pltpu.make_async_copy(zero_ref, _slab_rows(xp_ref, blk, MOE_BLOCK * ns), sem)

    def issue(g, carry):
        for u in range(ISSUE_UNROLL):
            i = g * ISSUE_UNROLL + u
            copy(i, d1_ref[base + i]).start(priority=0)
            copy(i, d2_ref[base + i]).start(priority=1)
        return carry

    def drain(g, carry):
        for _ in range(2 * ISSUE_UNROLL):
            copy(0, 0).wait()
        return carry

    lax.fori_loop(0, tr // ISSUE_UNROLL, issue, 0)
    wgb_ref[...] = wg_ref[...].astype(BF16)
    wub_ref[...] = wu_ref[...].astype(BF16)
    wdb_ref[...] = wd_ref[...].astype(BF16)
    lax.fori_loop(0, tr // ISSUE_UNROLL, drain, 0)

    @pl.when(step == pl.num_programs(0) - 1)
    def _():
        zero_ref[...] = jnp.zeros_like(zero_ref)

        def pad_expert(e, n_pad):
            n = sz_ref[e]
            first = ps_ref[e] + n
            n_e = (MOE_BLOCK - n % MOE_BLOCK) % MOE_BLOCK

            def one(r, c):
                pad_copy(first + r).start()
                return c

            lax.fori_loop(0, n_e, one, 0)
            return n_pad + n_e

        n_pad = lax.fori_loop(0, N_EXPERTS, pad_expert, 0)

        def drain_pad(r, c):
            pad_copy(0).wait()
            return c

        lax.fori_loop(0, n_pad, drain_pad, 0)

        first_blk = (ps_ref[N_EXPERTS - 1] + sz_ref[N_EXPERTS - 1] + MOE_BLOCK - 1) // MOE_BLOCK
        n_blocks = xp_ref.shape[0] // (MOE_BLOCK * ns)

        def tail_start(blk, c):
            pad_block_copy(blk).start()
            return c

        def tail_wait(blk, c):
            pad_block_copy(blk).wait()
            return c

        lax.fori_loop(first_blk, n_blocks, tail_start, 0)
        lax.fori_loop(first_blk, n_blocks, tail_wait, 0)


def _dispatch(pstart, sizes, dest1, dest2, hn3_slab, wg, wu, wd, n_rows, ns):
    t = dest1.shape[0]
    n_e, d, ff = wg.shape
    tr = t // n_e
    assert tr * n_e == t and tr % ISSUE_UNROLL == 0
    per_expert = lambda r, c: pl.BlockSpec((None, r, c), lambda i, *_: (i, 0, 0))
    return pl.pallas_call(
        functools.partial(_dispatch_kernel, tr=tr, ns=ns),
        grid_spec=pltpu.PrefetchScalarGridSpec(
            num_scalar_prefetch=4,
            grid=(n_e,),
            in_specs=[pl.BlockSpec((tr * ns, LANES), lambda i, *_: (i, 0)),
                      per_expert(d, ff), per_expert(d, ff), per_expert(ff, d)],
            out_specs=[pl.BlockSpec(memory_space=pl.ANY),
                       per_expert(d, ff), per_expert(d, ff), per_expert(ff, d)],
            scratch_shapes=[pltpu.VMEM((MOE_BLOCK * ns, LANES), U32), pltpu.SemaphoreType.DMA],
        ),
        out_shape=[jax.ShapeDtypeStruct((n_rows * ns, LANES), U32),
                   jax.ShapeDtypeStruct(wg.shape, BF16), jax.ShapeDtypeStruct(wu.shape, BF16),
                   jax.ShapeDtypeStruct(wd.shape, BF16)],
        compiler_params=_cparams(1),
        name="dispatch",
    )(pstart, sizes, dest1, dest2, hn3_slab, wg, wu, wd)


BLOCKS_PER_STEP = 4


def _expert_kernel(be_ref, nu_ref, x_ref, *refs):
    del be_ref
    w_refs, y_ref = refs[:-1], refs[-1]
    rows = y_ref.shape[0] // BLOCKS_PER_STEP
    n_live = nu_ref[0] - pl.program_id(0) * BLOCKS_PER_STEP

    def mlp(j):
        wg_ref, wu_ref, wd_ref = w_refs[3 * j:3 * j + 3]
        x = _from_slab(x_ref.at[pl.ds(j * rows, rows)], MOE_BLOCK).astype(BF16)
        hmid = _silu(_dot(x, wg_ref[...])) * _dot(x, wu_ref[...])
        return _dot(hmid, wd_ref[...])

    for live in range(BLOCKS_PER_STEP + 1):
        if live == 0:
            cond = n_live <= 0
        elif live == BLOCKS_PER_STEP:
            cond = n_live >= live
        else:
            cond = n_live == live

        @pl.when(cond)
        def _(live=live):
            ys = [mlp(j) for j in range(live)]
            for j in range(BLOCKS_PER_STEP):
                out = y_ref.at[pl.ds(j * rows, rows)]
                if j < live:
                    _to_slab(out, ys[j])
                else:
                    out[...] = jnp.zeros_like(out)


def _experts(blk_e, n_used, x_pad, wg, wu, wd):
    d, ff = wg.shape[-2:]
    blk = MOE_BLOCK * _slab_rows_per_token(d)
    n_steps = x_pad.shape[0] // (blk * BLOCKS_PER_STEP)
    assert n_steps * blk * BLOCKS_PER_STEP == x_pad.shape[0]
    rows = lambda i, be, nu: (jnp.minimum(i, (nu[0] - 1) // BLOCKS_PER_STEP), 0)
    weights = lambda j, r, c: pl.BlockSpec((None, r, c), lambda i, be, nu: (be[i * BLOCKS_PER_STEP + j], 0, 0))
    w_specs, w_args = [], []
    for j in range(BLOCKS_PER_STEP):
        w_specs += [weights(j, d, ff), weights(j, d, ff), weights(j, ff, d)]
        w_args += [wg, wu, wd]
    return pl.pallas_call(
        _expert_kernel,
        grid_spec=pltpu.PrefetchScalarGridSpec(
            num_scalar_prefetch=2,
            grid=(n_steps,),
            in_specs=[pl.BlockSpec((blk * BLOCKS_PER_STEP, LANES), rows)] + w_specs,
            out_specs=pl.BlockSpec((blk * BLOCKS_PER_STEP, LANES), lambda i, be, nu: (i, 0)),
        ),
        out_shape=jax.ShapeDtypeStruct(x_pad.shape, U32),
        compiler_params=_cparams(1),
        name="experts",
    )(blk_e, n_used, x_pad, *w_args)


def _combine_kernel(d1_ref, d2_ref, h2_ref, wt_ref, fnw_ref, y_ref, out_ref, b1_ref, b2_ref, sem, *, tr, ns):
    step = pl.program_id(0)
    n_steps = pl.num_programs(0)

    def copy(d, buf, slot, i):
        return pltpu.make_async_copy(_slab_rows(y_ref, d, ns), _slab_rows(buf.at[slot], i, ns), sem.at[slot])

    def issue_step(st):
        slot = st % 2

        def issue(g, carry):
            for u in range(ISSUE_UNROLL):
                i = g * ISSUE_UNROLL + u
                copy(d1_ref[st * tr + i], b1_ref, slot, i).start(priority=0)
                copy(d2_ref[st * tr + i], b2_ref, slot, i).start(priority=1)
            return carry

        lax.fori_loop(0, tr // ISSUE_UNROLL, issue, 0)

    @pl.when(step == 0)
    def _():
        issue_step(step)

    @pl.when(step + 1 < n_steps)
    def _():
        issue_step(step + 1)

    slot = step % 2

    def drain(g, carry):
        for _ in range(ISSUE_UNROLL):
            copy(0, b1_ref, slot, 0).wait()
            copy(0, b2_ref, slot, 0).wait()
        return carry

    lax.fori_loop(0, tr // ISSUE_UNROLL, drain, 0)

    wt = wt_ref[...].T
    h3 = (h2_ref[...] + wt[:, 0:1] * _from_slab(b1_ref.at[slot], tr)
          + wt[:, 1:2] * _from_slab(b2_ref.at[slot], tr))
    out_ref[...] = _rms(h3) * fnw_ref[...]


def _combine(dest1, dest2, h2, wts, fnw, y_pad, tr):
    t, d = h2.shape
    ns = _slab_rows_per_token(d)
    return pl.pallas_call(
        functools.partial(_combine_kernel, tr=tr, ns=ns),
        grid_spec=pltpu.PrefetchScalarGridSpec(
            num_scalar_prefetch=2,
            grid=(t // tr,),
            in_specs=[pl.BlockSpec((tr, d), lambda i, *_: (i, 0)),
                      pl.BlockSpec((8, tr), lambda i, *_: (0, i)),
                      pl.BlockSpec((1, d), lambda i, *_: (0, 0)),
                      pl.BlockSpec(memory_space=pl.ANY)],
            out_specs=pl.BlockSpec((tr, d), lambda i, *_: (i, 0)),
            scratch_shapes=[pltpu.VMEM((2, tr * ns, LANES), U32), pltpu.VMEM((2, tr * ns, LANES), U32),
                            pltpu.SemaphoreType.DMA((2,))],
        ),
        out_shape=jax.ShapeDtypeStruct((t, d), F32),
        compiler_params=_cparams(1),
        name="combine",
    )(dest1, dest2, h2, wts, fnw, y_pad)


def _pick(n, pref):
    while n % pref:
        pref //= 2
    return pref


def kernel(x, mem, norm_mix_w, w_in, conv_w, gdn_a_log, gdn_dt_bias, gdn_out_norm_w, hgrn_lb, hgrn_out_norm_w, w_branch_a, w_branch_b, w_out, norm_xattn_w, norm_mem_w, xattn_wq, xattn_wkv, xattn_wo, norm_ffn_w, router_group_w, router_group_b, router_expert_w, router_expert_b, expert_w_gate, expert_w_up, expert_w_down, final_norm_w):
    b, s, d = x.shape
    t = b * s
    depth = w_in.shape[0]
    w = HEADS * HEAD_DIM
    qkv_w = 3 * w
    assert d == w and s % CHUNK == 0

    tc = _pick(s, 256)
    tm_proj = _pick(t, 1024)
    tm_post = _pick(s, 512)
    tr = _pick(t, 256)

    def pad_lanes(v, offset=0):
        return jnp.zeros((1, LANES), F32).at[0, offset:offset + v.shape[0]].set(v.astype(F32))

    assert depth == 1
    h3d = x
    for layer in range(depth):
        w_main_t, w_small_t = _wprep(w_in[layer].T, qkv_w, 2 * HEADS, 1024)
        p_main, p_small = _proj(h3d.reshape(t, d), norm_mix_w[layer][None, :], w_main_t, w_small_t, tm_proj, 2048)
        p3 = p_main.reshape(b, s, -1)
        o_a = _gdn(p3, p_small.reshape(b, s, LANES), conv_w[layer], pad_lanes(gdn_a_log[layer]),
                   pad_lanes(gdn_dt_bias[layer]), tc)
        o_b = _hgrn(p3, hgrn_lb[layer:], tc)
        kv = _kv(mem, norm_mem_w[layer][None, :], xattn_wkv[layer].astype(BF16))
        w_router = jnp.zeros((d, LANES), F32)
        w_router = w_router.at[:, :N_GROUPS].set(router_group_w[layer])
        w_router = w_router.at[:, N_GROUPS:N_GROUPS + N_EXPERTS].set(router_expert_w[layer])
        b_router = pad_lanes(router_group_b[layer]) + pad_lanes(router_expert_b[layer], N_GROUPS)
        tile8 = lambda v: jnp.tile(v.astype(F32), HEADS)[None, :]
        h2, hn3, logits_t = _post(
            h3d, o_a, o_b, p3, kv, tile8(gdn_out_norm_w[layer]), tile8(hgrn_out_norm_w[layer]),
            w_branch_a[layer].astype(BF16), w_branch_b[layer].astype(BF16), w_out[layer].astype(BF16),
            norm_xattn_w[layer][None, :], xattn_wq[layer].astype(BF16), xattn_wo[layer].astype(BF16),
            norm_ffn_w[layer][None, :], w_router, b_router, tm_post)

        m = t * 2
        n_rows = ((m + MOE_BLOCK - 1) // MOE_BLOCK) * MOE_BLOCK + N_EXPERTS * MOE_BLOCK
        n_blocks = -(-(n_rows // MOE_BLOCK) // BLOCKS_PER_STEP) * BLOCKS_PER_STEP
        n_rows = n_blocks * MOE_BLOCK
        dest, wts, blk_tab, exp_tab = _route(logits_t, _pick(t, 1024), n_blocks)
        dest1, dest2 = dest[0], dest[1]
        blk_e, n_used = blk_tab[0, :n_blocks], blk_tab[1, :1]
        sizes = exp_tab[0, N_GROUPS:N_GROUPS + N_EXPERTS]
        pstart = exp_tab[1, N_GROUPS:N_GROUPS + N_EXPERTS]

        x_pad, wg, wu, wd = _dispatch(pstart, sizes, dest1, dest2, hn3, expert_w_gate[layer], expert_w_up[layer],
                                      expert_w_down[layer], n_rows, _slab_rows_per_token(d))
        y_pad = _experts(blk_e, n_used, x_pad, wg, wu, wd)
        out = _combine(dest1, dest2, h2.reshape(t, d), wts, final_norm_w[None, :], y_pad, tr)
        h3d = out.reshape(b, s, d)
    return h3d
```

```python
import functools

import jax
import jax.numpy as jnp
from jax import lax
from jax.experimental import pallas as pl
from jax.experimental.pallas import tpu as pltpu

F32 = jnp.float32
BF16 = jnp.bfloat16
I32 = jnp.int32
U32 = jnp.uint32

EPS = 1e-6
CHUNK = 64
HEADS = 8
HEAD_DIM = 128
CONV_K = 4
XA_HEADS = 4
N_GROUPS = 4
EXP_PER_GROUP = 8
N_EXPERTS = N_GROUPS * EXP_PER_GROUP
MOE_BLOCK = 256
LANES = 128

VMEM_LIMIT = 52 * 1024 * 1024


def _cparams(n_axes):
    return pltpu.CompilerParams(dimension_semantics=("arbitrary",) * n_axes,
                                vmem_limit_bytes=VMEM_LIMIT)


def _dot(a, b):
    return jnp.dot(a.astype(BF16), b.astype(BF16), preferred_element_type=F32)


def _dot_nt(a, b):
    return lax.dot_general(a.astype(BF16), b.astype(BF16), (((1,), (1,)), ((), ())),
                           preferred_element_type=F32)


def _dot_tn(a, b):
    return lax.dot_general(a.astype(BF16), b.astype(BF16), (((0,), (0,)), ((), ())),
                           preferred_element_type=F32)


def _split(x):
    hi = x.astype(BF16)
    lo = (x - hi.astype(F32)).astype(BF16)
    return hi, lo


def _dot_exact_lhs(m_bf16, x):
    hi, lo = _split(x)
    return (jnp.dot(m_bf16, hi, preferred_element_type=F32)
            + jnp.dot(m_bf16, lo, preferred_element_type=F32))


def _rms(x):
    return x * lax.rsqrt(jnp.mean(x * x, axis=-1, keepdims=True) + EPS)


def _silu(x):
    return x * jax.nn.sigmoid(x)


def _softplus(x):
    return jnp.maximum(x, 0.0) + jnp.log(1.0 + jnp.exp(-jnp.abs(x)))


HIGH_HALF = 0xFFFF0000


def _slab_rows_per_token(d):
    return d // (2 * LANES)


def _to_slab(ref, x):
    n, d = x.shape
    ns = _slab_rows_per_token(d)
    bits = lambda v: pltpu.bitcast(v.astype(BF16).astype(F32), U32)
    for s in range(ns):
        lo = bits(x[:, s * LANES:(s + 1) * LANES])
        hi = bits(x[:, (s + ns) * LANES:(s + ns + 1) * LANES])
        ref[pl.ds(s, n, stride=ns), :] = (lo >> 16) | (hi & jnp.uint32(HIGH_HALF))


def _from_slab(ref, n):
    ns = ref.shape[0] // n
    words = [ref[pl.ds(s, n, stride=ns), :] for s in range(ns)]
    lo = [pltpu.bitcast(wd << 16, F32) for wd in words]
    hi = [pltpu.bitcast(wd & jnp.uint32(HIGH_HALF), F32) for wd in words]
    return jnp.concatenate(lo + hi, axis=1)


def _chunk_masks(tc):
    ri = lax.broadcasted_iota(I32, (tc, tc), 0)
    ci = lax.broadcasted_iota(I32, (tc, tc), 1)
    same = (ri // CHUNK) == (ci // CHUNK)
    causal = same & (ri >= ci)
    strict = same & (ri > ci)
    return ri, ci, causal, strict


def _wprep_kernel(wt_ref, main_ref, small_ref, tail_ref, *, skip_blk, n_small):
    i = pl.program_id(0)
    last = pl.num_programs(0) - 1
    rows, d = wt_ref.shape
    keep = rows - n_small

    @pl.when(i < skip_blk)
    def _():
        main_ref[...] = wt_ref[...].astype(BF16)

    @pl.when(i == skip_blk)
    def _():
        small_ref[...] = jnp.concatenate([wt_ref[0:n_small, :].astype(BF16),
                                          jnp.zeros((LANES - n_small, d), BF16)], axis=0)

    @pl.when(i > skip_blk)
    def _():
        main_ref[0:keep, :] = tail_ref[0:keep, :]
        main_ref[keep:rows, :] = wt_ref[0:n_small, :].astype(BF16)

    @pl.when((i >= skip_blk) & (i < last))
    def _():
        tail_ref[0:keep, :] = wt_ref[n_small:rows, :].astype(BF16)


def _wprep(w_t, skip_row, n_small, rows):
    n_in, d = w_t.shape
    assert skip_row % rows == 0 and (n_in - n_small) % rows == 0 and n_small % 16 == 0
    skip_blk = skip_row // rows
    n_out_blk = (n_in - n_small) // rows
    out_blk = lambda i: (jnp.where(i <= skip_blk, jnp.minimum(i, skip_blk - 1), i - 1), 0)
    return pl.pallas_call(
        functools.partial(_wprep_kernel, skip_blk=skip_blk, n_small=n_small),
        grid=(n_out_blk + 1,),
        in_specs=[pl.BlockSpec((rows, d), lambda i: (i, 0))],
        out_specs=[pl.BlockSpec((rows, d), out_blk), pl.BlockSpec((LANES, d), lambda i: (0, 0))],
        out_shape=[jax.ShapeDtypeStruct((n_in - n_small, d), BF16), jax.ShapeDtypeStruct((LANES, d), BF16)],
        scratch_shapes=[pltpu.VMEM((rows, d), BF16)],
        compiler_params=_cparams(1),
        name="wprep",
    )(w_t)


def _proj_kernel(x_ref, nw_ref, w_ref, ws_ref, out_ref, small_ref, hn_ref):
    nt = (((1,), (1,)), ((), ()))

    @pl.when(pl.program_id(1) == 0)
    def _():
        hn = (_rms(x_ref[...]) * nw_ref[...]).astype(BF16)
        hn_ref[...] = hn
        small_ref[...] = lax.dot_general(hn, ws_ref[...], nt, preferred_element_type=F32)

    out_ref[...] = lax.dot_general(hn_ref[...], w_ref[...], nt, preferred_element_type=F32).astype(BF16)


def _proj(x2, nw, w_main_t, w_small_t, tm, tn):
    t, d = x2.shape
    n = w_main_t.shape[0]
    return pl.pallas_call(
        _proj_kernel,
        grid=(t // tm, n // tn),
        in_specs=[
            pl.BlockSpec((tm, d), lambda i, j: (i, 0)),
            pl.BlockSpec((1, d), lambda i, j: (0, 0)),
            pl.BlockSpec((tn, d), lambda i, j: (j, 0)),
            pl.BlockSpec((LANES, d), lambda i, j: (0, 0)),
        ],
        out_specs=[
            pl.BlockSpec((tm, tn), lambda i, j: (i, j)),
            pl.BlockSpec((tm, LANES), lambda i, j: (i, 0)),
        ],
        out_shape=[jax.ShapeDtypeStruct((t, n), BF16), jax.ShapeDtypeStruct((t, LANES), F32)],
        scratch_shapes=[pltpu.VMEM((tm, d), BF16)],
        compiler_params=_cparams(2),
        name="proj",
    )(x2, nw, w_main_t, w_small_t)


def _gdn_kernel(q_ref, k_ref, v_ref, sm_ref, cw_ref, alog_ref, dtb_ref, o_ref, xs_ref, st_ref, *, tc):
    w = HEADS * HEAD_DIM
    nc = tc // CHUNK

    @pl.when(pl.program_id(1) == 0)
    def _():
        xs_ref[0:8, :] = jnp.zeros((8, 3 * w), F32)
        st_ref[...] = jnp.zeros_like(st_ref)

    _, _, causal, strict = _chunk_masks(tc)
    tri = jnp.where(causal, 1.0, 0.0).astype(BF16)
    wi = lax.broadcasted_iota(I32, (CHUNK, tc), 0)
    wj = lax.broadcasted_iota(I32, (CHUNK, tc), 1)
    eye_w = jnp.where(wi == wj % CHUNK, 1.0, 0.0)
    blk_w = wj // CHUNK

    def fold(m_bd):
        acc = m_bd[0:CHUNK]
        for c in range(1, nc):
            acc = acc + m_bd[c * CHUNK:(c + 1) * CHUNK]
        return acc

    def unfold(m_w):
        return jnp.concatenate([jnp.where(blk_w == c, m_w, 0.0) for c in range(nc)], axis=0)

    sm = sm_ref[...]
    lane = lax.broadcasted_iota(I32, (tc, LANES), 1)
    g_all = jnp.where(lane < HEADS, -jnp.exp(alog_ref[...]) * _softplus(sm + dtb_ref[...]), 0.0)
    beta_all = jax.nn.sigmoid(sm)
    cum = _dot_exact_lhs(tri, g_all)
    ecum = jnp.exp(cum)
    cum_t = cum.T

    sr = lax.broadcasted_iota(I32, ((CONV_K - 1) * tc, tc), 0)
    sc = lax.broadcasted_iota(I32, ((CONV_K - 1) * tc, tc), 1)
    assert tc & (tc - 1) == 0
    shifts = jnp.where(sc == (sr & (tc - 1)) - ((sr >> (tc.bit_length() - 1)) + 1), 1.0, 0.0).astype(BF16)

    def conv_part(p, ref):
        cols = slice(p * w, (p + 1) * w)
        xb = ref[...]
        shifted = jnp.dot(shifts, xb, preferred_element_type=F32)
        acc = cw_ref[CONV_K - 1:CONV_K, cols] * xb.astype(F32)
        for s in range(1, CONV_K):
            acc = acc + cw_ref[CONV_K - 1 - s:CONV_K - s, cols] * shifted[(s - 1) * tc:s * tc]
        xs_ref[8:16, cols] = xb[0:8].astype(F32)
        first = cw_ref[CONV_K - 1:CONV_K, cols] * xs_ref[8:16, cols]
        for s in range(1, CONV_K):
            first = first + cw_ref[CONV_K - 1 - s:CONV_K - s, cols] * xs_ref[8 - s:16 - s, cols]
        xs_ref[0:8, cols] = xb[tc - 8:tc].astype(F32)
        return _silu(jnp.concatenate([first, acc[8:]], axis=0))

    qkv = [conv_part(p, ref) for p, ref in enumerate((q_ref, k_ref, v_ref))]

    def l2n(x):
        return x * lax.rsqrt(jnp.sum(x * x, axis=-1, keepdims=True) + EPS)

    heads = range(HEADS)
    ks, a_bd, qk_bd, rhs, qd, cc = [], [], [], [], [], []
    for h in heads:
        hc = slice(h * HEAD_DIM, (h + 1) * HEAD_DIM)
        q = l2n(qkv[0][:, hc]) * HEAD_DIM ** -0.5
        k = l2n(qkv[1][:, hc])
        v = qkv[2][:, hc]
        cch = cum[:, h:h + 1]
        cr = cum_t[h:h + 1, :]
        bc = beta_all[:, HEADS + h:HEADS + h + 1]
        ec = ecum[:, h:h + 1]
        dec = jnp.where(causal, jnp.exp(jnp.where(causal, cch - cr, 0.0)), 0.0)
        kb = k * bc
        a_bd.append(jnp.where(strict, _dot_nt(kb, k) * dec, 0.0))
        qk_bd.append(_dot_nt(q, k) * dec)
        rhs.append(jnp.concatenate([v * bc, kb * ec], axis=1).astype(BF16))
        qd.append(q * ec)
        ks.append(k)
        cc.append(cch)

    xw, pw = [], []
    for h in heads:
        aw = fold(a_bd[h])
        xw.append(eye_w - aw)
        pw.append(_dot(aw, a_bd[h]))
    n_sq = CHUNK.bit_length() - 2
    for it in range(n_sq):
        for h in heads:
            p_bd = unfold(pw[h]).astype(BF16)
            if it + 1 < n_sq:
                res = _dot(jnp.concatenate([xw[h], pw[h]], axis=0), p_bd)
                xw[h] = xw[h] + res[:CHUNK]
                pw[h] = res[CHUNK:]
            else:
                xw[h] = xw[h] + _dot(xw[h], p_bd)

    uw, qp, o0 = [], [], []
    for h in heads:
        uwh = _dot(unfold(xw[h]), rhs[h])
        qkuw = _dot(qk_bd[h], uwh)
        uw.append(uwh.astype(BF16))
        o0.append(qkuw[:, :HEAD_DIM])
        qp.append(qd[h] - qkuw[:, HEAD_DIM:])

    state = [st_ref[h] for h in heads]
    for c in range(nc):
        r0 = c * CHUNK
        rows = slice(r0, r0 + CHUNK)
        kuw, dlast = [], []
        for h in heads:
            last = cc[h][r0 + CHUNK - 1:r0 + CHUNK, :]
            kd = ks[h][rows] * jnp.exp(last - cc[h][rows])
            kuw.append(_dot_tn(kd, uw[h][rows]))
            dlast.append(jnp.exp(last))
        for h in heads:
            s = state[h]
            res = _dot(jnp.concatenate([kuw[h][:, HEAD_DIM:], qp[h][rows]], axis=0), s)
            o = res[HEAD_DIM:] + o0[h][rows]
            state[h] = s * dlast[h] - res[:HEAD_DIM] + kuw[h][:, :HEAD_DIM]
            o_ref[rows, h * HEAD_DIM:(h + 1) * HEAD_DIM] = _rms(o).astype(BF16)
    for h in heads:
        st_ref[h] = state[h]


def _gdn(p3, small3, conv_w, alog, dtb, tc):
    b, s, _ = p3.shape
    w = HEADS * HEAD_DIM
    return pl.pallas_call(
        functools.partial(_gdn_kernel, tc=tc),
        grid=(b, s // tc),
        in_specs=[
            pl.BlockSpec((None, tc, w), lambda i, j: (i, j, 0)),
            pl.BlockSpec((None, tc, w), lambda i, j: (i, j, 1)),
            pl.BlockSpec((None, tc, w), lambda i, j: (i, j, 2)),
            pl.BlockSpec((None, tc, LANES), lambda i, j: (i, j, 0)),
            pl.BlockSpec((CONV_K, 3 * w), lambda i, j: (0, 0)),
            pl.BlockSpec((1, LANES), lambda i, j: (0, 0)),
            pl.BlockSpec((1, LANES), lambda i, j: (0, 0)),
        ],
        out_specs=pl.BlockSpec((None, tc, w), lambda i, j: (i, j, 0)),
        out_shape=jax.ShapeDtypeStruct((b, s, w), BF16),
        scratch_shapes=[pltpu.VMEM((16, 3 * w), F32), pltpu.VMEM((HEADS, HEAD_DIM, HEAD_DIM), F32)],
        compiler_params=_cparams(2),
        name="gdn",
    )(p3, p3, p3, small3, conv_w, alog, dtb)


def _hgrn_kernel(f_ref, q_ref, i_ref, lb_ref, wg_ref, wu_ref, wd_ref, o_ref, wgb_ref, wub_ref, wdb_ref, st_ref, *, tc):
    nc = tc // CHUNK

    @pl.when(pl.program_id(1) == 0)
    def _():
        st_ref[...] = jnp.zeros_like(st_ref)

    wgb_ref[...] = wg_ref[...].astype(BF16)
    wub_ref[...] = wu_ref[...].astype(BF16)
    wdb_ref[...] = wd_ref[...].astype(BF16)

    _, _, causal, _ = _chunk_masks(tc)
    tri = jnp.where(causal, 1.0, 0.0).astype(BF16)

    lbp = lb_ref[...]
    lbe = jnp.exp(lbp - jnp.max(lbp, axis=0, keepdims=True))
    lb = lbe[0:1, :] / jnp.sum(lbe, axis=0, keepdims=True)

    heads = range(HEADS)
    hcols = [slice(h * HEAD_DIM, (h + 1) * HEAD_DIM) for h in heads]
    forget = lb + (1.0 - lb) * jax.nn.sigmoid(f_ref[...].astype(F32))
    cum = _dot_exact_lhs(tri, jnp.log(forget))
    kk = 1.0 - forget
    q_in = (q_ref[...].astype(F32) * HEAD_DIM ** -0.5 * jnp.exp(cum)).astype(BF16)
    k_in_f = kk * jnp.exp(-cum)
    k_in = k_in_f.astype(BF16)
    v = i_ref[...]
    intra = [jnp.where(causal, _dot_nt(q_in[:, hc], k_in[:, hc]), 0.0).astype(BF16) for hc in hcols]
    state = [st_ref[h] for h in heads]
    for c in range(nc):
        r0 = c * CHUNK
        rows = slice(r0, r0 + CHUNK)
        last = cum[r0 + CHUNK - 1:r0 + CHUNK, :]
        dlast = jnp.exp(last)
        k_dec = (k_in_f[rows] * dlast).astype(BF16)
        for h, hc in zip(heads, hcols):
            o = _dot_nt(q_in[rows, hc], state[h]) + _dot(intra[h][rows, rows], v[rows, hc])
            state[h] = state[h] * dlast[:, hc] + _dot_tn(v[rows, hc], k_dec[:, hc])
            o_ref[rows, hc] = _rms(o).astype(BF16)
    for h in heads:
        st_ref[h] = state[h]


def _hgrn(p3, lb_logits, expert_ws, tc):
    b, s, _ = p3.shape
    w = HEADS * HEAD_DIM
    nj = s // tc
    n_steps = b * nj
    flat = [wt.reshape(-1, wt.shape[-1]) for wt in expert_ws]
    slab = lambda f: pl.BlockSpec((f.shape[0] // n_steps, f.shape[1]), lambda i, j: (i * nj + j, 0))
    for f in flat:
        assert f.shape[0] % n_steps == 0 and (f.shape[0] // n_steps) % 16 == 0
    outs = pl.pallas_call(
        functools.partial(_hgrn_kernel, tc=tc),
        grid=(b, nj),
        in_specs=[
            pl.BlockSpec((None, tc, w), lambda i, j: (i, j, 4)),
            pl.BlockSpec((None, tc, w), lambda i, j: (i, j, 5)),
            pl.BlockSpec((None, tc, w), lambda i, j: (i, j, 6)),
            pl.BlockSpec(lb_logits.shape, lambda i, j: (0, 0)),
        ] + [slab(f) for f in flat],
        out_specs=[pl.BlockSpec((None, tc, w), lambda i, j: (i, j, 0))] + [slab(f) for f in flat],
        out_shape=[jax.ShapeDtypeStruct((b, s, w), BF16)] + [jax.ShapeDtypeStruct(f.shape, BF16) for f in flat],
        scratch_shapes=[pltpu.VMEM((HEADS, HEAD_DIM, HEAD_DIM), F32)],
        compiler_params=_cparams(2),
        name="hgrn",
    )(p3, p3, p3, lb_logits, *flat)
    return outs[0], [o.reshape(wt.shape) for o, wt in zip(outs[1:], expert_ws)]


def _kv_kernel(mem_ref, nw_ref, wkv_ref, kv_ref):
    mn = (_rms(mem_ref[...]) * nw_ref[...]).astype(BF16)
    kv_ref[...] = jnp.dot(mn, wkv_ref[...], preferred_element_type=F32).astype(BF16)


def _kv(mem, nw, wkv):
    b, m, d = mem.shape
    return pl.pallas_call(
        _kv_kernel,
        grid=(b,),
        in_specs=[
            pl.BlockSpec((None, m, d), lambda i: (i, 0, 0)),
            pl.BlockSpec((1, d), lambda i: (0, 0)),
            pl.BlockSpec(wkv.shape, lambda i: (0, 0)),
        ],
        out_specs=pl.BlockSpec((None, m, 2 * d), lambda i: (i, 0, 0)),
        out_shape=jax.ShapeDtypeStruct((b, m, 2 * d), BF16),
        compiler_params=_cparams(1),
        name="kv",
    )(mem, nw, wkv)


def _post_kernel(x_ref, oa_ref, ob_ref, oga_ref, ogb_ref, ga_ref, gb_ref, kv_ref,
                 gnw_ref, hnw_ref, wa_ref, wb_ref, wout_ref, nx_ref, wq_ref, wo_ref, nf_ref,
                 wr_ref, br_ref, h2_ref, hn3_ref, lg_ref):
    d = x_ref.shape[-1]
    dh = d // XA_HEADS
    ya = oa_ref[...].astype(F32) * gnw_ref[...] * _silu(oga_ref[...].astype(F32))
    yb = ob_ref[...].astype(F32) * hnw_ref[...] * _silu(ogb_ref[...].astype(F32))
    merged = (jax.nn.sigmoid(ga_ref[...].astype(F32)) * _dot(ya, wa_ref[...])
              + jax.nn.sigmoid(gb_ref[...].astype(F32)) * _dot(yb, wb_ref[...]))
    h1 = x_ref[...] + _dot(merged, wout_ref[...])

    q = _dot(_rms(h1) * nx_ref[...], wq_ref[...]) * dh ** -0.5
    outs = []
    for hh in range(XA_HEADS):
        kh = kv_ref[:, hh * dh:(hh + 1) * dh]
        vh = kv_ref[:, d + hh * dh:d + (hh + 1) * dh]
        sc = _dot_nt(q[:, hh * dh:(hh + 1) * dh], kh)
        p = jnp.exp(sc - jnp.max(sc, axis=-1, keepdims=True))
        outs.append(_dot(p, vh) / jnp.sum(p, axis=-1, keepdims=True))
    h2 = h1 + _dot(jnp.concatenate(outs, axis=1), wo_ref[...])
    h2_ref[...] = h2

    hn3 = _rms(h2) * nf_ref[...]
    _to_slab(hn3_ref, hn3)
    hi, lo = _split(hn3)
    whi, wlo = _split(wr_ref[...])
    hw = jnp.dot(hi, jnp.concatenate([whi, wlo], axis=1), preferred_element_type=F32)
    lg = (hw[:, :LANES] + hw[:, LANES:] + jnp.dot(lo, whi, preferred_element_type=F32)) + br_ref[...]
    lg_ref[...] = lg.T[0:lg_ref.shape[0], :]


def _post(x3, oa, ob, p3, kv, gnw, hnw, wa, wb, wout, nx, wq, wo, nf, wr, br, tm):
    b, s, d = x3.shape
    ns = _slab_rows_per_token(d)
    row = lambda c: pl.BlockSpec((None, tm, d), lambda i, j: (i, j, c))
    full = lambda a: pl.BlockSpec(a.shape, lambda i, j: (0,) * a.ndim, pipeline_mode=pl.Buffered(1))
    return pl.pallas_call(
        _post_kernel,
        grid=(b, s // tm),
        in_specs=[row(0), row(0), row(0), row(3), row(7), row(8), row(9),
                  pl.BlockSpec((None,) + kv.shape[1:], lambda i, j: (i, 0, 0)),
                  full(gnw), full(hnw), full(wa), full(wb), full(wout), full(nx), full(wq), full(wo),
                  full(nf), full(wr), full(br)],
        out_specs=[row(0), pl.BlockSpec((tm * ns, LANES), lambda i, j: (i * (s // tm) + j, 0)),
                   pl.BlockSpec((ROUTE_ROWS, tm), lambda i, j: (0, i * (s // tm) + j))],
        out_shape=[jax.ShapeDtypeStruct((b, s, d), F32), jax.ShapeDtypeStruct((b * s * ns, LANES), U32),
                   jax.ShapeDtypeStruct((ROUTE_ROWS, b * s), F32)],
        compiler_params=_cparams(2),
        name="post",
    )(x3, oa, ob, p3, p3, p3, p3, kv, gnw, hnw, wa, wb, wout, nx, wq, wo, nf, wr, br)


ROUTE_ROWS = 40


def _route_kernel(lg_ref, dest_ref, wt_ref, blk_ref, exp_ref, code_ref, carry_ref, *, tr, t, n_blk_lanes):
    step = pl.program_id(0)
    rr = ROUTE_ROWS

    @pl.when(step == 0)
    def _():
        carry_ref[...] = jnp.zeros_like(carry_ref)

    neg, big = -1e30, 1e9
    lt = lg_ref[...]
    row_f = lax.broadcasted_iota(I32, (rr, tr), 0).astype(F32)
    cmax = lambda v: jnp.max(v, axis=0, keepdims=True)
    csum = lambda v: jnp.sum(v, axis=0, keepdims=True)
    first = lambda m: jnp.min(jnp.where(m, row_f, big), axis=0, keepdims=True)

    is_g = row_f < N_GROUPS
    gl = jnp.where(is_g, lt, neg)
    gmax = cmax(gl)
    gidx = first(gl == gmax)
    g_p = 1.0 / csum(jnp.where(is_g, jnp.exp(gl - gmax), 0.0))

    lo = N_GROUPS + gidx * EXP_PER_GROUP
    in_grp = (row_f >= lo) & (row_f < lo + EXP_PER_GROUP)
    el = jnp.where(in_grp, lt, neg)
    m1 = cmax(el)
    i1 = first(el == m1)
    el2 = jnp.where(row_f == i1, neg, el)
    m2 = cmax(el2)
    i2 = first(el2 == m2)
    esum = csum(jnp.where(in_grp, jnp.exp(el - m1), 0.0))
    p1 = 1.0 / esum
    p2 = jnp.exp(m2 - m1) / esum
    w1 = g_p * p1 / (p1 + p2)
    w2 = g_p * p2 / (p1 + p2)

    oh1 = row_f == i1
    oh2 = row_f == i2
    both = jnp.where(oh1 | oh2, 1.0, 0.0)
    sub = min(tr, MOE_BLOCK)
    ti = lax.broadcasted_iota(I32, (sub, sub), 0)
    tj = lax.broadcasted_iota(I32, (sub, sub), 1)
    earlier = jnp.where(ti < tj, 1.0, 0.0).astype(BF16)
    carry = carry_ref[...]
    r1, r2 = [], []
    for sb in range(tr // sub):
        cols = slice(sb * sub, (sb + 1) * sub)
        before = jnp.dot(both[:, cols].astype(BF16), earlier, preferred_element_type=F32) + carry[:, 0:1]
        r1.append(csum(jnp.where(oh1[:, cols], before, 0.0)))
        r2.append(csum(jnp.where(oh2[:, cols], before, 0.0)))
        carry = carry + jnp.sum(both[:, cols], axis=1, keepdims=True)
    carry_ref[...] = carry
    r1 = jnp.concatenate(r1, axis=1)
    r2 = jnp.concatenate(r2, axis=1)

    row8 = lax.broadcasted_iota(I32, (8, tr), 0)
    rows4 = lambda a, b, c, d: jnp.where(row8 == 0, a, jnp.where(row8 == 1, b, jnp.where(row8 == 2, c,
                                                                                      jnp.where(row8 == 3, d, 0.0))))
    wt_ref[...] = rows4(w1, w2, 0.0, 0.0)
    code_ref[:, pl.ds(pl.multiple_of(step * tr, tr), tr)] = rows4(i1, i2, r1, r2)

    @pl.when(step == pl.num_programs(0) - 1)
    def _():
        blk = float(MOE_BLOCK)
        padded = jnp.floor((carry + (blk - 1.0)) * (1.0 / blk)) * blk
        ri = lax.broadcasted_iota(I32, (rr, rr), 0)
        rj = lax.broadcasted_iota(I32, (rr, rr), 1)
        lower = jnp.where(rj <= ri, 1.0, 0.0).astype(BF16)
        hi = jnp.floor(padded * (1.0 / 16384.0))
        rem = padded - hi * 16384.0
        mid = jnp.floor(rem * (1.0 / 128.0))
        low = rem - mid * 128.0
        psum = lambda v: jnp.dot(lower, v.astype(BF16), preferred_element_type=F32)
        pend = psum(hi) * 16384.0 + psum(mid) * 128.0 + psum(low)
        pstart = pend - padded

        bstart = lax.broadcasted_iota(I32, (rr, n_blk_lanes), 1).astype(F32) * blk
        ended = csum(jnp.where(pend[:, 0:1] <= bstart, 1.0, 0.0))
        blk_e = jnp.clip(ended - N_GROUPS, 0.0, N_EXPERTS - 1.0)
        n_used = pend[rr - 1:rr, 0:1] * (1.0 / blk)
        row8b = lax.broadcasted_iota(I32, (8, n_blk_lanes), 0)
        blk_ref[...] = jnp.where(row8b == 0, blk_e, jnp.where(row8b == 1, n_used, 0.0)).astype(I32)

        diag = lax.broadcasted_iota(I32, (rr, LANES), 0) == lax.broadcasted_iota(I32, (rr, LANES), 1)
        to_lanes = lambda v: csum(jnp.where(diag, v, 0.0))
        row8e = lax.broadcasted_iota(I32, (8, LANES), 0)
        exp_ref[...] = jnp.where(row8e == 0, to_lanes(carry),
                                 jnp.where(row8e == 1, to_lanes(pstart), 0.0)).astype(I32)

        ch = min(t, 2048)
        row_c = lax.broadcasted_iota(I32, (rr, ch), 0).astype(F32)
        row8c = lax.broadcasted_iota(I32, (8, ch), 0)

        def chunk(c, carry_):
            sl = pl.ds(pl.multiple_of(c * ch, ch), ch)
            cd = code_ref[:, sl]
            dest = lambda s: csum(jnp.where(row_c == cd[s:s + 1], pstart[:, 0:1], 0.0)) + cd[s + 2:s + 3]
            dest_ref[:, sl] = jnp.where(row8c == 0, dest(0), jnp.where(row8c == 1, dest(1), 0.0)).astype(I32)
            return carry_

        lax.fori_loop(0, t // ch, chunk, 0)


def _route(logits_t, tr, n_blocks):
    t = logits_t.shape[1]
    n_blk_lanes = -(-n_blocks // LANES) * LANES
    whole = lambda shape: pl.BlockSpec(shape, lambda i: (0, 0))
    return pl.pallas_call(
        functools.partial(_route_kernel, tr=tr, t=t, n_blk_lanes=n_blk_lanes),
        grid=(t // tr,),
        in_specs=[pl.BlockSpec((ROUTE_ROWS, tr), lambda i: (0, i))],
        out_specs=[whole((8, t)), pl.BlockSpec((8, tr), lambda i: (0, i)), whole((8, n_blk_lanes)), whole((8, LANES))],
        out_shape=[jax.ShapeDtypeStruct((8, t), I32), jax.ShapeDtypeStruct((8, t), F32),
                   jax.ShapeDtypeStruct((8, n_blk_lanes), I32), jax.ShapeDtypeStruct((8, LANES), I32)],
        scratch_shapes=[pltpu.VMEM((8, t), F32), pltpu.VMEM((ROUTE_ROWS, LANES), F32)],
        compiler_params=_cparams(1),
        name="route",
    )(logits_t)


ISSUE_UNROLL = 8


def _slab_rows(ref, row, ns):
    return ref.at[pl.ds(pl.multiple_of(row * ns, ns), ns)]


def _dispatch_kernel(ps_ref, sz_ref, d1_ref, d2_ref, hn_ref, xp_ref, zero_ref, sem, *, tr, ns):
    step = pl.program_id(0)
    base = step * tr

    def copy(i, d):
        return pltpu.make_async_copy(_slab_rows(hn_ref, i, ns), _slab_rows(xp_ref, d, ns), sem)

    def pad_copy(d):
        return pltpu.make_async_copy(zero_ref.at[pl.ds(0, ns)], _slab_rows(xp_ref, d, ns), sem)

    def pad_block_copy(blk):
        return pltpu.make_async_copy(zero_ref, _slab_rows(xp_ref, blk, MOE_BLOCK * ns), sem)

    def issue(g, carry):
        for u in range(ISSUE_UNROLL):
            i = g * ISSUE_UNROLL + u
            copy(i, d1_ref[base + i]).start(priority=0)
            copy(i, d2_ref[base + i]).start(priority=1)
        return carry

    def drain(g, carry):
        for _ in range(2 * ISSUE_UNROLL):
            copy(0, 0).wait()
        return carry

    lax.fori_loop(0, tr // ISSUE_UNROLL, issue, 0)
    lax.fori_loop(0, tr // ISSUE_UNROLL, drain, 0)

    @pl.when(step == pl.num_programs(0) - 1)
    def _():
        zero_ref[...] = jnp.zeros_like(zero_ref)

        def pad_expert(e, n_pad):
            n = sz_ref[e]
            first = ps_ref[e] + n
            n_e = (MOE_BLOCK - n % MOE_BLOCK) % MOE_BLOCK

            def one(r, c):
                pad_copy(first + r).start()
                return c

            lax.fori_loop(0, n_e, one, 0)
            return n_pad + n_e

        n_pad = lax.fori_loop(0, N_EXPERTS, pad_expert, 0)

        def drain_pad(r, c):
            pad_copy(0).wait()
            return c

        lax.fori_loop(0, n_pad, drain_pad, 0)

        first_blk = (ps_ref[N_EXPERTS - 1] + sz_ref[N_EXPERTS - 1] + MOE_BLOCK - 1) // MOE_BLOCK
        n_blocks = xp_ref.shape[0] // (MOE_BLOCK * ns)

        def tail_start(blk, c):
            pad_block_copy(blk).start()
            return c

        def tail_wait(blk, c):
            pad_block_copy(blk).wait()
            return c

        lax.fori_loop(first_blk, n_blocks, tail_start, 0)
        lax.fori_loop(first_blk, n_blocks, tail_wait, 0)


def _dispatch(pstart, sizes, dest1, dest2, hn3_slab, n_rows, tr, ns):
    t = dest1.shape[0]
    assert t % tr == 0 and tr % ISSUE_UNROLL == 0
    return pl.pallas_call(
        functools.partial(_dispatch_kernel, tr=tr, ns=ns),
        grid_spec=pltpu.PrefetchScalarGridSpec(
            num_scalar_prefetch=4,
            grid=(t // tr,),
            in_specs=[pl.BlockSpec((tr * ns, LANES), lambda i, *_: (i, 0))],
            out_specs=pl.BlockSpec(memory_space=pl.ANY),
            scratch_shapes=[pltpu.VMEM((MOE_BLOCK * ns, LANES), U32), pltpu.SemaphoreType.DMA],
        ),
        out_shape=jax.ShapeDtypeStruct((n_rows * ns, LANES), U32),
        compiler_params=_cparams(1),
        name="dispatch",
    )(pstart, sizes, dest1, dest2, hn3_slab)


BLOCKS_PER_STEP = 4


def _expert_kernel(be_ref, nu_ref, x_ref, *refs):
    del be_ref
    w_refs, y_ref = refs[:-1], refs[-1]
    rows = y_ref.shape[0] // BLOCKS_PER_STEP
    n_live = nu_ref[0] - pl.program_id(0) * BLOCKS_PER_STEP

    def mlp(j):
        wg_ref, wu_ref, wd_ref = w_refs[3 * j:3 * j + 3]
        x = _from_slab(x_ref.at[pl.ds(j * rows, rows)], MOE_BLOCK).astype(BF16)
        hmid = _silu(_dot(x, wg_ref[...])) * _dot(x, wu_ref[...])
        return _dot(hmid, wd_ref[...])

    for live in range(BLOCKS_PER_STEP + 1):
        if live == 0:
            cond = n_live <= 0
        elif live == BLOCKS_PER_STEP:
            cond = n_live >= live
        else:
            cond = n_live == live

        @pl.when(cond)
        def _(live=live):
            ys = [mlp(j) for j in range(live)]
            for j in range(BLOCKS_PER_STEP):
                out = y_ref.at[pl.ds(j * rows, rows)]
                if j < live:
                    _to_slab(out, ys[j])
                else:
                    out[...] = jnp.zeros_like(out)


def _experts(blk_e, n_used, x_pad, wg, wu, wd):
    d, ff = wg.shape[-2:]
    blk = MOE_BLOCK * _slab_rows_per_token(d)
    n_steps = x_pad.shape[0] // (blk * BLOCKS_PER_STEP)
    assert n_steps * blk * BLOCKS_PER_STEP == x_pad.shape[0]
    rows = lambda i, be, nu: (jnp.minimum(i, (nu[0] - 1) // BLOCKS_PER_STEP), 0)
    weights = lambda j, r, c: pl.BlockSpec((None, r, c), lambda i, be, nu: (be[i * BLOCKS_PER_STEP + j], 0, 0))
    w_specs, w_args = [], []
    for j in range(BLOCKS_PER_STEP):
        w_specs += [weights(j, d, ff), weights(j, d, ff), weights(j, ff, d)]
        w_args += [wg, wu, wd]
    return pl.pallas_call(
        _expert_kernel,
        grid_spec=pltpu.PrefetchScalarGridSpec(
            num_scalar_prefetch=2,
            grid=(n_steps,),
            in_specs=[pl.BlockSpec((blk * BLOCKS_PER_STEP, LANES), rows)] + w_specs,
            out_specs=pl.BlockSpec((blk * BLOCKS_PER_STEP, LANES), lambda i, be, nu: (i, 0)),
        ),
        out_shape=jax.ShapeDtypeStruct(x_pad.shape, U32),
        compiler_params=_cparams(1),
        name="experts",
    )(blk_e, n_used, x_pad, *w_args)


def _combine_kernel(d1_ref, d2_ref, h2_ref, wt_ref, fnw_ref, y_ref, out_ref, b1_ref, b2_ref, sem, *, tr, ns):
    step = pl.program_id(0)
    n_steps = pl.num_programs(0)

    def copy(d, buf, slot, i):
        return pltpu.make_async_copy(_slab_rows(y_ref, d, ns), _slab_rows(buf.at[slot], i, ns), sem.at[slot])

    def issue_step(st):
        slot = st % 2

        def issue(g, carry):
            for u in range(ISSUE_UNROLL):
                i = g * ISSUE_UNROLL + u
                copy(d1_ref[st * tr + i], b1_ref, slot, i).start(priority=0)
                copy(d2_ref[st * tr + i], b2_ref, slot, i).start(priority=1)
            return carry

        lax.fori_loop(0, tr // ISSUE_UNROLL, issue, 0)

    @pl.when(step == 0)
    def _():
        issue_step(step)

    @pl.when(step + 1 < n_steps)
    def _():
        issue_step(step + 1)

    slot = step % 2

    def drain(g, carry):
        for _ in range(ISSUE_UNROLL):
            copy(0, b1_ref, slot, 0).wait()
            copy(0, b2_ref, slot, 0).wait()
        return carry

    lax.fori_loop(0, tr // ISSUE_UNROLL, drain, 0)

    wt = wt_ref[...].T
    h3 = (h2_ref[...] + wt[:, 0:1] * _from_slab(b1_ref.at[slot], tr)
          + wt[:, 1:2] * _from_slab(b2_ref.at[slot], tr))
    out_ref[...] = _rms(h3) * fnw_ref[...]


def _combine(dest1, dest2, h2, wts, fnw, y_pad, tr):
    t, d = h2.shape
    ns = _slab_rows_per_token(d)
    return pl.pallas_call(
        functools.partial(_combine_kernel, tr=tr, ns=ns),
        grid_spec=pltpu.PrefetchScalarGridSpec(
            num_scalar_prefetch=2,
            grid=(t // tr,),
            in_specs=[pl.BlockSpec((tr, d), lambda i, *_: (i, 0)),
                      pl.BlockSpec((8, tr), lambda i, *_: (0, i)),
                      pl.BlockSpec((1, d), lambda i, *_: (0, 0)),
                      pl.BlockSpec(memory_space=pl.ANY)],
            out_specs=pl.BlockSpec((tr, d), lambda i, *_: (i, 0)),
            scratch_shapes=[pltpu.VMEM((2, tr * ns, LANES), U32), pltpu.VMEM((2, tr * ns, LANES), U32),
                            pltpu.SemaphoreType.DMA((2,))],
        ),
        out_shape=jax.ShapeDtypeStruct((t, d), F32),
        compiler_params=_cparams(1),
        name="combine",
    )(dest1, dest2, h2, wts, fnw, y_pad)


def _pick(n, pref):
    while n % pref:
        pref //= 2
    return pref


def kernel(x, mem, norm_mix_w, w_in, conv_w, gdn_a_log, gdn_dt_bias, gdn_out_norm_w, hgrn_lb, hgrn_out_norm_w, w_branch_a, w_branch_b, w_out, norm_xattn_w, norm_mem_w, xattn_wq, xattn_wkv, xattn_wo, norm_ffn_w, router_group_w, router_group_b, router_expert_w, router_expert_b, expert_w_gate, expert_w_up, expert_w_down, final_norm_w):
    b, s, d = x.shape
    t = b * s
    depth = w_in.shape[0]
    w = HEADS * HEAD_DIM
    qkv_w = 3 * w
    assert d == w and s % CHUNK == 0

    tc = _pick(s, 256)
    tm_proj = _pick(t, 1024)
    tm_post = _pick(s, 512)
    tr = _pick(t, 256)

    def pad_lanes(v, offset=0):
        return jnp.zeros((1, LANES), F32).at[0, offset:offset + v.shape[0]].set(v.astype(F32))

    assert depth == 1
    h3d = x
    for layer in range(depth):
        w_main_t, w_small_t = _wprep(w_in[layer].T, qkv_w, 2 * HEADS, 1024)
        p_main, p_small = _proj(h3d.reshape(t, d), norm_mix_w[layer][None, :], w_main_t, w_small_t, tm_proj, 2048)
        p3 = p_main.reshape(b, s, -1)
        o_a = _gdn(p3, p_small.reshape(b, s, LANES), conv_w[layer], pad_lanes(gdn_a_log[layer]),
                   pad_lanes(gdn_dt_bias[layer]), tc)
        o_b, (wg, wu, wd) = _hgrn(p3, hgrn_lb[layer:],
                                  (expert_w_gate[layer], expert_w_up[layer], expert_w_down[layer]), tc)
        kv = _kv(mem, norm_mem_w[layer][None, :], xattn_wkv[layer].astype(BF16))
        w_router = jnp.zeros((d, LANES), F32)
        w_router = w_router.at[:, :N_GROUPS].set(router_group_w[layer])
        w_router = w_router.at[:, N_GROUPS:N_GROUPS + N_EXPERTS].set(router_expert_w[layer])
        b_router = pad_lanes(router_group_b[layer]) + pad_lanes(router_expert_b[layer], N_GROUPS)
        tile8 = lambda v: jnp.tile(v.astype(F32), HEADS)[None, :]
        h2, hn3, logits_t = _post(
            h3d, o_a, o_b, p3, kv, tile8(gdn_out_norm_w[layer]), tile8(hgrn_out_norm_w[layer]),
            w_branch_a[layer].astype(BF16), w_branch_b[layer].astype(BF16), w_out[layer].astype(BF16),
            norm_xattn_w[layer][None, :], xattn_wq[layer].astype(BF16), xattn_wo[layer].astype(BF16),
            norm_ffn_w[layer][None, :], w_router, b_router, tm_post)

        m = t * 2
        n_rows = ((m + MOE_BLOCK - 1) // MOE_BLOCK) * MOE_BLOCK + N_EXPERTS * MOE_BLOCK
        n_blocks = -(-(n_rows // MOE_BLOCK) // BLOCKS_PER_STEP) * BLOCKS_PER_STEP
        n_rows = n_blocks * MOE_BLOCK
        dest, wts, blk_tab, exp_tab = _route(logits_t, _pick(t, 1024), n_blocks)
        dest1, dest2 = dest[0], dest[1]
        blk_e, n_used = blk_tab[0, :n_blocks], blk_tab[1, :1]
        sizes = exp_tab[0, N_GROUPS:N_GROUPS + N_EXPERTS]
        pstart = exp_tab[1, N_GROUPS:N_GROUPS + N_EXPERTS]

        x_pad = _dispatch(pstart, sizes, dest1, dest2, hn3, n_rows, _pick(t, 1024), _slab_rows_per_token(d))
        y_pad = _experts(blk_e, n_used, x_pad, wg, wu, wd)
        out = _combine(dest1, dest2, h2.reshape(t, d), wts, final_norm_w[None, :], y_pad, tr)
        h3d = out.reshape(b, s, d)
    return h3d
```

```python
import functools

import jax
import jax.numpy as jnp
from jax import lax
from jax.experimental import pallas as pl
from jax.experimental.pallas import tpu as pltpu

F32 = jnp.float32
BF16 = jnp.bfloat16
I32 = jnp.int32
U32 = jnp.uint32

EPS = 1e-6
CHUNK = 64
HEADS = 8
HEAD_DIM = 128
CONV_K = 4
XA_HEADS = 4
N_GROUPS = 4
EXP_PER_GROUP = 8
N_EXPERTS = N_GROUPS * EXP_PER_GROUP
MOE_BLOCK = 256
LANES = 128

VMEM_LIMIT = 52 * 1024 * 1024


def _cparams(n_axes):
    return pltpu.CompilerParams(dimension_semantics=("arbitrary",) * n_axes,
                                vmem_limit_bytes=VMEM_LIMIT)


def _dot(a, b):
    return jnp.dot(a.astype(BF16), b.astype(BF16), preferred_element_type=F32)


def _dot_nt(a, b):
    return lax.dot_general(a.astype(BF16), b.astype(BF16), (((1,), (1,)), ((), ())),
                           preferred_element_type=F32)


def _dot_tn(a, b):
    return lax.dot_general(a.astype(BF16), b.astype(BF16), (((0,), (0,)), ((), ())),
                           preferred_element_type=F32)


def _split(x):
    hi = x.astype(BF16)
    lo = (x - hi.astype(F32)).astype(BF16)
    return hi, lo


def _dot_exact_lhs(m_bf16, x):
    hi, lo = _split(x)
    return (jnp.dot(m_bf16, hi, preferred_element_type=F32)
            + jnp.dot(m_bf16, lo, preferred_element_type=F32))


def _rms(x):
    return x * lax.rsqrt(jnp.mean(x * x, axis=-1, keepdims=True) + EPS)


def _silu(x):
    return x * jax.nn.sigmoid(x)


def _softplus(x):
    return jnp.maximum(x, 0.0) + jnp.log(1.0 + jnp.exp(-jnp.abs(x)))


HIGH_HALF = 0xFFFF0000


def _slab_rows_per_token(d):
    return d // (2 * LANES)


def _to_slab(ref, x):
    n, d = x.shape
    ns = _slab_rows_per_token(d)
    bits = lambda v: pltpu.bitcast(v.astype(BF16).astype(F32), U32)
    for s in range(ns):
        lo = bits(x[:, s * LANES:(s + 1) * LANES])
        hi = bits(x[:, (s + ns) * LANES:(s + ns + 1) * LANES])
        ref[pl.ds(s, n, stride=ns), :] = (lo >> 16) | (hi & jnp.uint32(HIGH_HALF))


def _from_slab(ref, n):
    ns = ref.shape[0] // n
    words = [ref[pl.ds(s, n, stride=ns), :] for s in range(ns)]
    lo = [pltpu.bitcast(wd << 16, F32) for wd in words]
    hi = [pltpu.bitcast(wd & jnp.uint32(HIGH_HALF), F32) for wd in words]
    return jnp.concatenate(lo + hi, axis=1)


def _chunk_masks(tc):
    ri = lax.broadcasted_iota(I32, (tc, tc), 0)
    ci = lax.broadcasted_iota(I32, (tc, tc), 1)
    same = (ri // CHUNK) == (ci // CHUNK)
    causal = same & (ri >= ci)
    strict = same & (ri > ci)
    return ri, ci, causal, strict


def _wprep_kernel(wt_ref, main_ref, small_ref, tail_ref, *, skip_blk, n_small):
    i = pl.program_id(0)
    last = pl.num_programs(0) - 1
    rows, d = wt_ref.shape
    keep = rows - n_small

    @pl.when(i < skip_blk)
    def _():
        main_ref[...] = wt_ref[...].astype(BF16)

    @pl.when(i == skip_blk)
    def _():
        small_ref[...] = jnp.concatenate([wt_ref[0:n_small, :].astype(BF16),
                                          jnp.zeros((LANES - n_small, d), BF16)], axis=0)

    @pl.when(i > skip_blk)
    def _():
        main_ref[0:keep, :] = tail_ref[0:keep, :]
        main_ref[keep:rows, :] = wt_ref[0:n_small, :].astype(BF16)

    @pl.when((i >= skip_blk) & (i < last))
    def _():
        tail_ref[0:keep, :] = wt_ref[n_small:rows, :].astype(BF16)


def _wprep(w_t, skip_row, n_small, rows):
    n_in, d = w_t.shape
    assert skip_row % rows == 0 and (n_in - n_small) % rows == 0 and n_small % 16 == 0
    skip_blk = skip_row // rows
    n_out_blk = (n_in - n_small) // rows
    out_blk = lambda i: (jnp.where(i <= skip_blk, jnp.minimum(i, skip_blk - 1), i - 1), 0)
    return pl.pallas_call(
        functools.partial(_wprep_kernel, skip_blk=skip_blk, n_small=n_small),
        grid=(n_out_blk + 1,),
        in_specs=[pl.BlockSpec((rows, d), lambda i: (i, 0))],
        out_specs=[pl.BlockSpec((rows, d), out_blk), pl.BlockSpec((LANES, d), lambda i: (0, 0))],
        out_shape=[jax.ShapeDtypeStruct((n_in - n_small, d), BF16), jax.ShapeDtypeStruct((LANES, d), BF16)],
        scratch_shapes=[pltpu.VMEM((rows, d), BF16)],
        compiler_params=_cparams(1),
        name="wprep",
    )(w_t)


def _proj_kernel(x_ref, nw_ref, w_ref, ws_ref, out_ref, small_ref, hn_ref):
    nt = (((1,), (1,)), ((), ()))

    @pl.when(pl.program_id(1) == 0)
    def _():
        hn = (_rms(x_ref[...]) * nw_ref[...]).astype(BF16)
        hn_ref[...] = hn
        small_ref[...] = lax.dot_general(hn, ws_ref[...], nt, preferred_element_type=F32)

    out_ref[...] = lax.dot_general(hn_ref[...], w_ref[...], nt, preferred_element_type=F32).astype(BF16)


def _proj(x2, nw, w_main_t, w_small_t, tm, tn):
    t, d = x2.shape
    n = w_main_t.shape[0]
    return pl.pallas_call(
        _proj_kernel,
        grid=(t // tm, n // tn),
        in_specs=[
            pl.BlockSpec((tm, d), lambda i, j: (i, 0)),
            pl.BlockSpec((1, d), lambda i, j: (0, 0)),
            pl.BlockSpec((tn, d), lambda i, j: (j, 0)),
            pl.BlockSpec((LANES, d), lambda i, j: (0, 0)),
        ],
        out_specs=[
            pl.BlockSpec((tm, tn), lambda i, j: (i, j)),
            pl.BlockSpec((tm, LANES), lambda i, j: (i, 0)),
        ],
        out_shape=[jax.ShapeDtypeStruct((t, n), BF16), jax.ShapeDtypeStruct((t, LANES), F32)],
        scratch_shapes=[pltpu.VMEM((tm, d), BF16)],
        compiler_params=_cparams(2),
        name="proj",
    )(x2, nw, w_main_t, w_small_t)


def _gdn_kernel(q_ref, k_ref, v_ref, sm_ref, cw_ref, alog_ref, dtb_ref, o_ref, xs_ref, st_ref, *, tc):
    w = HEADS * HEAD_DIM
    nc = tc // CHUNK

    @pl.when(pl.program_id(1) == 0)
    def _():
        xs_ref[0:8, :] = jnp.zeros((8, 3 * w), F32)
        st_ref[...] = jnp.zeros_like(st_ref)

    _, _, causal, strict = _chunk_masks(tc)
    tri = jnp.where(causal, 1.0, 0.0).astype(BF16)
    wi = lax.broadcasted_iota(I32, (CHUNK, tc), 0)
    wj = lax.broadcasted_iota(I32, (CHUNK, tc), 1)
    eye_w = jnp.where(wi == wj % CHUNK, 1.0, 0.0)
    blk_w = wj // CHUNK

    def fold(m_bd):
        acc = m_bd[0:CHUNK]
        for c in range(1, nc):
            acc = acc + m_bd[c * CHUNK:(c + 1) * CHUNK]
        return acc

    def unfold(m_w):
        return jnp.concatenate([jnp.where(blk_w == c, m_w, 0.0) for c in range(nc)], axis=0)

    sm = sm_ref[...]
    lane = lax.broadcasted_iota(I32, (tc, LANES), 1)
    g_all = jnp.where(lane < HEADS, -jnp.exp(alog_ref[...]) * _softplus(sm + dtb_ref[...]), 0.0)
    beta_all = jax.nn.sigmoid(sm)
    cum = _dot_exact_lhs(tri, g_all)
    ecum = jnp.exp(cum)
    cum_t = cum.T

    sr = lax.broadcasted_iota(I32, ((CONV_K - 1) * tc, tc), 0)
    sc = lax.broadcasted_iota(I32, ((CONV_K - 1) * tc, tc), 1)
    assert tc & (tc - 1) == 0
    shifts = jnp.where(sc == (sr & (tc - 1)) - ((sr >> (tc.bit_length() - 1)) + 1), 1.0, 0.0).astype(BF16)

    def conv_part(p, ref):
        cols = slice(p * w, (p + 1) * w)
        xb = ref[...]
        shifted = jnp.dot(shifts, xb, preferred_element_type=F32)
        acc = cw_ref[CONV_K - 1:CONV_K, cols] * xb.astype(F32)
        for s in range(1, CONV_K):
            acc = acc + cw_ref[CONV_K - 1 - s:CONV_K - s, cols] * shifted[(s - 1) * tc:s * tc]
        xs_ref[8:16, cols] = xb[0:8].astype(F32)
        first = cw_ref[CONV_K - 1:CONV_K, cols] * xs_ref[8:16, cols]
        for s in range(1, CONV_K):
            first = first + cw_ref[CONV_K - 1 - s:CONV_K - s, cols] * xs_ref[8 - s:16 - s, cols]
        xs_ref[0:8, cols] = xb[tc - 8:tc].astype(F32)
        return _silu(jnp.concatenate([first, acc[8:]], axis=0))

    qkv = [conv_part(p, ref) for p, ref in enumerate((q_ref, k_ref, v_ref))]

    def l2n(x):
        return x * lax.rsqrt(jnp.sum(x * x, axis=-1, keepdims=True) + EPS)

    heads = range(HEADS)
    ks, a_bd, qk_bd, rhs, qd, cc = [], [], [], [], [], []
    for h in heads:
        hc = slice(h * HEAD_DIM, (h + 1) * HEAD_DIM)
        q = l2n(qkv[0][:, hc]) * HEAD_DIM ** -0.5
        k = l2n(qkv[1][:, hc])
        v = qkv[2][:, hc]
        cch = cum[:, h:h + 1]
        cr = cum_t[h:h + 1, :]
        bc = beta_all[:, HEADS + h:HEADS + h + 1]
        ec = ecum[:, h:h + 1]
        dec = jnp.where(causal, jnp.exp(jnp.where(causal, cch - cr, 0.0)), 0.0)
        kb = k * bc
        a_bd.append(jnp.where(strict, _dot_nt(kb, k) * dec, 0.0))
        qk_bd.append(_dot_nt(q, k) * dec)
        rhs.append(jnp.concatenate([v * bc, kb * ec], axis=1).astype(BF16))
        qd.append(q * ec)
        ks.append(k)
        cc.append(cch)

    xw, pw = [], []
    for h in heads:
        aw = fold(a_bd[h])
        xw.append(eye_w - aw)
        pw.append(_dot(aw, a_bd[h]))
    n_sq = CHUNK.bit_length() - 2
    for it in range(n_sq):
        for h in heads:
            p_bd = unfold(pw[h]).astype(BF16)
            if it + 1 < n_sq:
                res = _dot(jnp.concatenate([xw[h], pw[h]], axis=0), p_bd)
                xw[h] = xw[h] + res[:CHUNK]
                pw[h] = res[CHUNK:]
            else:
                xw[h] = xw[h] + _dot(xw[h], p_bd)

    uw, qp, o0 = [], [], []
    for h in heads:
        uwh = _dot(unfold(xw[h]), rhs[h])
        qkuw = _dot(qk_bd[h], uwh)
        uw.append(uwh.astype(BF16))
        o0.append(qkuw[:, :HEAD_DIM])
        qp.append(qd[h] - qkuw[:, HEAD_DIM:])

    state = [st_ref[h] for h in heads]
    for c in range(nc):
        r0 = c * CHUNK
        rows = slice(r0, r0 + CHUNK)
        kuw, dlast = [], []
        for h in heads:
            last = cc[h][r0 + CHUNK - 1:r0 + CHUNK, :]
            kd = ks[h][rows] * jnp.exp(last - cc[h][rows])
            kuw.append(_dot_tn(kd, uw[h][rows]))
            dlast.append(jnp.exp(last))
        for h in heads:
            s = state[h]
            res = _dot(jnp.concatenate([kuw[h][:, HEAD_DIM:], qp[h][rows]], axis=0), s)
            o = res[HEAD_DIM:] + o0[h][rows]
            state[h] = s * dlast[h] - res[:HEAD_DIM] + kuw[h][:, :HEAD_DIM]
            o_ref[rows, h * HEAD_DIM:(h + 1) * HEAD_DIM] = _rms(o).astype(BF16)
    for h in heads:
        st_ref[h] = state[h]


def _gdn(p3, small3, conv_w, alog, dtb, tc):
    b, s, _ = p3.shape
    w = HEADS * HEAD_DIM
    return pl.pallas_call(
        functools.partial(_gdn_kernel, tc=tc),
        grid=(b, s // tc),
        in_specs=[
            pl.BlockSpec((None, tc, w), lambda i, j: (i, j, 0)),
            pl.BlockSpec((None, tc, w), lambda i, j: (i, j, 1)),
            pl.BlockSpec((None, tc, w), lambda i, j: (i, j, 2)),
            pl.BlockSpec((None, tc, LANES), lambda i, j: (i, j, 0)),
            pl.BlockSpec((CONV_K, 3 * w), lambda i, j: (0, 0)),
            pl.BlockSpec((1, LANES), lambda i, j: (0, 0)),
            pl.BlockSpec((1, LANES), lambda i, j: (0, 0)),
        ],
        out_specs=pl.BlockSpec((None, tc, w), lambda i, j: (i, j, 0)),
        out_shape=jax.ShapeDtypeStruct((b, s, w), BF16),
        scratch_shapes=[pltpu.VMEM((16, 3 * w), F32), pltpu.VMEM((HEADS, HEAD_DIM, HEAD_DIM), F32)],
        compiler_params=_cparams(2),
        name="gdn",
    )(p3, p3, p3, small3, conv_w, alog, dtb)


def _hgrn_kernel(f_ref, q_ref, i_ref, lb_ref, wg_ref, wu_ref, wd_ref, o_ref, wgb_ref, wub_ref, wdb_ref, st_ref, *, tc):
    nc = tc // CHUNK

    @pl.when(pl.program_id(1) == 0)
    def _():
        st_ref[...] = jnp.zeros_like(st_ref)

    wgb_ref[...] = wg_ref[...].astype(BF16)
    wub_ref[...] = wu_ref[...].astype(BF16)
    wdb_ref[...] = wd_ref[...].astype(BF16)

    _, _, causal, _ = _chunk_masks(tc)
    tri = jnp.where(causal, 1.0, 0.0).astype(BF16)

    lbp = lb_ref[...]
    lbe = jnp.exp(lbp - jnp.max(lbp, axis=0, keepdims=True))
    lb = lbe[0:1, :] / jnp.sum(lbe, axis=0, keepdims=True)

    heads = range(HEADS)
    hcols = [slice(h * HEAD_DIM, (h + 1) * HEAD_DIM) for h in heads]
    forget = lb + (1.0 - lb) * jax.nn.sigmoid(f_ref[...].astype(F32))
    cum = _dot_exact_lhs(tri, jnp.log(forget))
    kk = 1.0 - forget
    q_in = (q_ref[...].astype(F32) * HEAD_DIM ** -0.5 * jnp.exp(cum)).astype(BF16)
    k_in_f = kk * jnp.exp(-cum)
    k_in = k_in_f.astype(BF16)
    v = i_ref[...]
    intra = [jnp.where(causal, _dot_nt(q_in[:, hc], k_in[:, hc]), 0.0).astype(BF16) for hc in hcols]
    state = [st_ref[h] for h in heads]
    for c in range(nc):
        r0 = c * CHUNK
        rows = slice(r0, r0 + CHUNK)
        last = cum[r0 + CHUNK - 1:r0 + CHUNK, :]
        dlast = jnp.exp(last)
        k_dec = (k_in_f[rows] * dlast).astype(BF16)
        for h, hc in zip(heads, hcols):
            o = _dot_nt(q_in[rows, hc], state[h]) + _dot(intra[h][rows, rows], v[rows, hc])
            state[h] = state[h] * dlast[:, hc] + _dot_tn(v[rows, hc], k_dec[:, hc])
            o_ref[rows, hc] = _rms(o).astype(BF16)
    for h in heads:
        st_ref[h] = state[h]


def _hgrn(p3, lb_logits, expert_ws, tc):
    b, s, _ = p3.shape
    w = HEADS * HEAD_DIM
    nj = s // tc
    n_steps = b * nj
    flat = [wt.reshape(-1, wt.shape[-1]) for wt in expert_ws]
    slab = lambda f: pl.BlockSpec((f.shape[0] // n_steps, f.shape[1]), lambda i, j: (i * nj + j, 0))
    for f in flat:
        assert f.shape[0] % n_steps == 0 and (f.shape[0] // n_steps) % 16 == 0
    outs = pl.pallas_call(
        functools.partial(_hgrn_kernel, tc=tc),
        grid=(b, nj),
        in_specs=[
            pl.BlockSpec((None, tc, w), lambda i, j: (i, j, 4)),
            pl.BlockSpec((None, tc, w), lambda i, j: (i, j, 5)),
            pl.BlockSpec((None, tc, w), lambda i, j: (i, j, 6)),
            pl.BlockSpec(lb_logits.shape, lambda i, j: (0, 0)),
        ] + [slab(f) for f in flat],
        out_specs=[pl.BlockSpec((None, tc, w), lambda i, j: (i, j, 0))] + [slab(f) for f in flat],
        out_shape=[jax.ShapeDtypeStruct((b, s, w), BF16)] + [jax.ShapeDtypeStruct(f.shape, BF16) for f in flat],
        scratch_shapes=[pltpu.VMEM((HEADS, HEAD_DIM, HEAD_DIM), F32)],
        compiler_params=_cparams(2),
        name="hgrn",
    )(p3, p3, p3, lb_logits, *flat)
    return outs[0], [o.reshape(wt.shape) for o, wt in zip(outs[1:], expert_ws)]


def _kv_kernel(mem_ref, nw_ref, wkv_ref, kv_ref):
    mn = (_rms(mem_ref[...]) * nw_ref[...]).astype(BF16)
    kv_ref[...] = jnp.dot(mn, wkv_ref[...], preferred_element_type=F32).astype(BF16)


def _kv(mem, nw, wkv):
    b, m, d = mem.shape
    return pl.pallas_call(
        _kv_kernel,
        grid=(b,),
        in_specs=[
            pl.BlockSpec((None, m, d), lambda i: (i, 0, 0)),
            pl.BlockSpec((1, d), lambda i: (0, 0)),
            pl.BlockSpec(wkv.shape, lambda i: (0, 0)),
        ],
        out_specs=pl.BlockSpec((None, m, 2 * d), lambda i: (i, 0, 0)),
        out_shape=jax.ShapeDtypeStruct((b, m, 2 * d), BF16),
        compiler_params=_cparams(1),
        name="kv",
    )(mem, nw, wkv)


def _post_kernel(x_ref, oa_ref, ob_ref, oga_ref, ogb_ref, ga_ref, gb_ref, kv_ref,
                 gnw_ref, hnw_ref, wa_ref, wb_ref, wout_ref, nx_ref, wq_ref, wo_ref, nf_ref,
                 wr_ref, br_ref, h2_ref, hn3_ref, lg_ref):
    d = x_ref.shape[-1]
    dh = d // XA_HEADS
    ya = oa_ref[...].astype(F32) * gnw_ref[...] * _silu(oga_ref[...].astype(F32))
    yb = ob_ref[...].astype(F32) * hnw_ref[...] * _silu(ogb_ref[...].astype(F32))
    merged = (jax.nn.sigmoid(ga_ref[...].astype(F32)) * _dot(ya, wa_ref[...])
              + jax.nn.sigmoid(gb_ref[...].astype(F32)) * _dot(yb, wb_ref[...]))
    h1 = x_ref[...] + _dot(merged, wout_ref[...])

    q = _dot(_rms(h1) * nx_ref[...], wq_ref[...]) * dh ** -0.5
    outs = []
    for hh in range(XA_HEADS):
        kh = kv_ref[:, hh * dh:(hh + 1) * dh]
        vh = kv_ref[:, d + hh * dh:d + (hh + 1) * dh]
        sc = _dot_nt(q[:, hh * dh:(hh + 1) * dh], kh)
        p = jnp.exp(sc - jnp.max(sc, axis=-1, keepdims=True))
        outs.append(_dot(p, vh) / jnp.sum(p, axis=-1, keepdims=True))
    h2 = h1 + _dot(jnp.concatenate(outs, axis=1), wo_ref[...])
    h2_ref[...] = h2

    hn3 = _rms(h2) * nf_ref[...]
    _to_slab(hn3_ref, hn3)
    hi, lo = _split(hn3)
    whi, wlo = _split(wr_ref[...])
    hw = jnp.dot(hi, jnp.concatenate([whi, wlo], axis=1), preferred_element_type=F32)
    lg = (hw[:, :LANES] + hw[:, LANES:] + jnp.dot(lo, whi, preferred_element_type=F32)) + br_ref[...]
    lg_ref[...] = lg.T[0:lg_ref.shape[0], :]


def _post(x3, oa, ob, p3, kv, gnw, hnw, wa, wb, wout, nx, wq, wo, nf, wr, br, tm):
    b, s, d = x3.shape
    ns = _slab_rows_per_token(d)
    row = lambda c: pl.BlockSpec((None, tm, d), lambda i, j: (i, j, c))
    full = lambda a: pl.BlockSpec(a.shape, lambda i, j: (0,) * a.ndim, pipeline_mode=pl.Buffered(1))
    return pl.pallas_call(
        _post_kernel,
        grid=(b, s // tm),
        in_specs=[row(0), row(0), row(0), row(3), row(7), row(8), row(9),
                  pl.BlockSpec((None,) + kv.shape[1:], lambda i, j: (i, 0, 0)),
                  full(gnw), full(hnw), full(wa), full(wb), full(wout), full(nx), full(wq), full(wo),
                  full(nf), full(wr), full(br)],
        out_specs=[row(0), pl.BlockSpec((tm * ns, LANES), lambda i, j: (i * (s // tm) + j, 0)),
                   pl.BlockSpec((ROUTE_ROWS, tm), lambda i, j: (0, i * (s // tm) + j))],
        out_shape=[jax.ShapeDtypeStruct((b, s, d), F32), jax.ShapeDtypeStruct((b * s * ns, LANES), U32),
                   jax.ShapeDtypeStruct((ROUTE_ROWS, b * s), F32)],
        compiler_params=_cparams(2),
        name="post",
    )(x3, oa, ob, p3, p3, p3, p3, kv, gnw, hnw, wa, wb, wout, nx, wq, wo, nf, wr, br)


ROUTE_ROWS = 40


def _route_kernel(lg_ref, dest_ref, wt_ref, blk_ref, exp_ref, code_ref, carry_ref, *, tr, t, n_blk_lanes):
    step = pl.program_id(0)
    rr = ROUTE_ROWS

    @pl.when(step == 0)
    def _():
        carry_ref[...] = jnp.zeros_like(carry_ref)

    neg, big = -1e30, 1e9
    lt = lg_ref[...]
    row_f = lax.broadcasted_iota(I32, (rr, tr), 0).astype(F32)
    cmax = lambda v: jnp.max(v, axis=0, keepdims=True)
    csum = lambda v: jnp.sum(v, axis=0, keepdims=True)
    first = lambda m: jnp.min(jnp.where(m, row_f, big), axis=0, keepdims=True)

    is_g = row_f < N_GROUPS
    gl = jnp.where(is_g, lt, neg)
    gmax = cmax(gl)
    gidx = first(gl == gmax)
    g_p = 1.0 / csum(jnp.where(is_g, jnp.exp(gl - gmax), 0.0))

    lo = N_GROUPS + gidx * EXP_PER_GROUP
    in_grp = (row_f >= lo) & (row_f < lo + EXP_PER_GROUP)
    el = jnp.where(in_grp, lt, neg)
    m1 = cmax(el)
    i1 = first(el == m1)
    el2 = jnp.where(row_f == i1, neg, el)
    m2 = cmax(el2)
    i2 = first(el2 == m2)
    esum = csum(jnp.where(in_grp, jnp.exp(el - m1), 0.0))
    p1 = 1.0 / esum
    p2 = jnp.exp(m2 - m1) / esum
    w1 = g_p * p1 / (p1 + p2)
    w2 = g_p * p2 / (p1 + p2)

    oh1 = row_f == i1
    oh2 = row_f == i2
    both = jnp.where(oh1 | oh2, 1.0, 0.0)
    sub = min(tr, MOE_BLOCK)
    ti = lax.broadcasted_iota(I32, (sub, sub), 0)
    tj = lax.broadcasted_iota(I32, (sub, sub), 1)
    earlier = jnp.where(ti < tj, 1.0, 0.0).astype(BF16)
    carry = carry_ref[...]
    r1, r2 = [], []
    for sb in range(tr // sub):
        cols = slice(sb * sub, (sb + 1) * sub)
        before = jnp.dot(both[:, cols].astype(BF16), earlier, preferred_element_type=F32) + carry[:, 0:1]
        r1.append(csum(jnp.where(oh1[:, cols], before, 0.0)))
        r2.append(csum(jnp.where(oh2[:, cols], before, 0.0)))
        carry = carry + jnp.sum(both[:, cols], axis=1, keepdims=True)
    carry_ref[...] = carry
    r1 = jnp.concatenate(r1, axis=1)
    r2 = jnp.concatenate(r2, axis=1)

    row8 = lax.broadcasted_iota(I32, (8, tr), 0)
    rows4 = lambda a, b, c, d: jnp.where(row8 == 0, a, jnp.where(row8 == 1, b, jnp.where(row8 == 2, c,
                                                                                      jnp.where(row8 == 3, d, 0.0))))
    wt_ref[...] = rows4(w1, w2, 0.0, 0.0)
    code_ref[:, pl.ds(pl.multiple_of(step * tr, tr), tr)] = rows4(i1, i2, r1, r2)

    @pl.when(step == pl.num_programs(0) - 1)
    def _():
        blk = float(MOE_BLOCK)
        padded = jnp.floor((carry + (blk - 1.0)) * (1.0 / blk)) * blk
        ri = lax.broadcasted_iota(I32, (rr, rr), 0)
        rj = lax.broadcasted_iota(I32, (rr, rr), 1)
        lower = jnp.where(rj <= ri, 1.0, 0.0).astype(BF16)
        hi = jnp.floor(padded * (1.0 / 16384.0))
        rem = padded - hi * 16384.0
        mid = jnp.floor(rem * (1.0 / 128.0))
        low = rem - mid * 128.0
        psum = lambda v: jnp.dot(lower, v.astype(BF16), preferred_element_type=F32)
        pend = psum(hi) * 16384.0 + psum(mid) * 128.0 + psum(low)
        pstart = pend - padded

        bstart = lax.broadcasted_iota(I32, (rr, n_blk_lanes), 1).astype(F32) * blk
        ended = csum(jnp.where(pend[:, 0:1] <= bstart, 1.0, 0.0))
        blk_e = jnp.clip(ended - N_GROUPS, 0.0, N_EXPERTS - 1.0)
        n_used = pend[rr - 1:rr, 0:1] * (1.0 / blk)
        row8b = lax.broadcasted_iota(I32, (8, n_blk_lanes), 0)
        blk_ref[...] = jnp.where(row8b == 0, blk_e, jnp.where(row8b == 1, n_used, 0.0)).astype(I32)

        diag = lax.broadcasted_iota(I32, (rr, LANES), 0) == lax.broadcasted_iota(I32, (rr, LANES), 1)
        to_lanes = lambda v: csum(jnp.where(diag, v, 0.0))
        row8e = lax.broadcasted_iota(I32, (8, LANES), 0)
        exp_ref[...] = jnp.where(row8e == 0, to_lanes(carry),
                                 jnp.where(row8e == 1, to_lanes(pstart), 0.0)).astype(I32)

        ch = min(t, 2048)
        row_c = lax.broadcasted_iota(I32, (rr, ch), 0).astype(F32)
        row8c = lax.broadcasted_iota(I32, (8, ch), 0)

        def chunk(c, carry_):
            sl = pl.ds(pl.multiple_of(c * ch, ch), ch)
            cd = code_ref[:, sl]
            dest = lambda s: csum(jnp.where(row_c == cd[s:s + 1], pstart[:, 0:1], 0.0)) + cd[s + 2:s + 3]
            dest_ref[:, sl] = jnp.where(row8c == 0, dest(0), jnp.where(row8c == 1, dest(1), 0.0)).astype(I32)
            return carry_

        lax.fori_loop(0, t // ch, chunk, 0)


def _route(logits_t, tr, n_blocks):
    t = logits_t.shape[1]
    n_blk_lanes = -(-n_blocks // LANES) * LANES
    whole = lambda shape: pl.BlockSpec(shape, lambda i: (0, 0))
    return pl.pallas_call(
        functools.partial(_route_kernel, tr=tr, t=t, n_blk_lanes=n_blk_lanes),
        grid=(t // tr,),
        in_specs=[pl.BlockSpec((ROUTE_ROWS, tr), lambda i: (0, i))],
        out_specs=[whole((8, t)), pl.BlockSpec((8, tr), lambda i: (0, i)), whole((8, n_blk_lanes)), whole((8, LANES))],
        out_shape=[jax.ShapeDtypeStruct((8, t), I32), jax.ShapeDtypeStruct((8, t), F32),
                   jax.ShapeDtypeStruct((8, n_blk_lanes), I32), jax.ShapeDtypeStruct((8, LANES), I32)],
        scratch_shapes=[pltpu.VMEM((8, t), F32), pltpu.VMEM((ROUTE_ROWS, LANES), F32)],
        compiler_params=_cparams(1),
        name="route",
    )(logits_t)


ISSUE_UNROLL = 8
GATHER_AHEAD = 2


def _slab_rows(ref, row, ns):
    return ref.at[pl.ds(pl.multiple_of(row * ns, ns), ns)]


def _dispatch_kernel(ps_ref, sz_ref, d1_ref, d2_ref, hn_ref, xp_ref, zero_ref, sem, *, tr, ns):
    step = pl.program_id(0)
    base = step * tr

    def copy(i, d):
        return pltpu.make_async_copy(_slab_rows(hn_ref, i, ns), _slab_rows(xp_ref, d, ns), sem)

    def pad_copy(d):
        return pltpu.make_async_copy(zero_ref.at[pl.ds(0, ns)], _slab_rows(xp_ref, d, ns), sem)

    def pad_block_copy(blk):
        return pltpu.make_async_copy(zero_ref, _slab_rows(xp_ref, blk, MOE_BLOCK * ns), sem)

    def issue(g, carry):
        for u in range(ISSUE_UNROLL):
            i = g * ISSUE_UNROLL + u
            copy(i, d1_ref[base + i]).start(priority=0)
            copy(i, d2_ref[base + i]).start(priority=1)
        return carry

    def drain(g, carry):
        for _ in range(2 * ISSUE_UNROLL):
            copy(0, 0).wait()
        return carry

    lax.fori_loop(0, tr // ISSUE_UNROLL, issue, 0)
    lax.fori_loop(0, tr // ISSUE_UNROLL, drain, 0)

    @pl.when(step == pl.num_programs(0) - 1)
    def _():
        zero_ref[...] = jnp.zeros_like(zero_ref)

        def pad_expert(e, n_pad):
            n = sz_ref[e]
            first = ps_ref[e] + n
            n_e = (MOE_BLOCK - n % MOE_BLOCK) % MOE_BLOCK

            def one(r, c):
                pad_copy(first + r).start()
                return c

            lax.fori_loop(0, n_e, one, 0)
            return n_pad + n_e

        n_pad = lax.fori_loop(0, N_EXPERTS, pad_expert, 0)

        def drain_pad(r, c):
            pad_copy(0).wait()
            return c

        lax.fori_loop(0, n_pad, drain_pad, 0)

        first_blk = (ps_ref[N_EXPERTS - 1] + sz_ref[N_EXPERTS - 1] + MOE_BLOCK - 1) // MOE_BLOCK
        n_blocks = xp_ref.shape[0] // (MOE_BLOCK * ns)

        def tail_start(blk, c):
            pad_block_copy(blk).start()
            return c

        def tail_wait(blk, c):
            pad_block_copy(blk).wait()
            return c

        lax.fori_loop(first_blk, n_blocks, tail_start, 0)
        lax.fori_loop(first_blk, n_blocks, tail_wait, 0)


def _dispatch(pstart, sizes, dest1, dest2, hn3_slab, n_rows, tr, ns):
    t = dest1.shape[0]
    assert t % tr == 0 and tr % ISSUE_UNROLL == 0
    return pl.pallas_call(
        functools.partial(_dispatch_kernel, tr=tr, ns=ns),
        grid_spec=pltpu.PrefetchScalarGridSpec(
            num_scalar_prefetch=4,
            grid=(t // tr,),
            in_specs=[pl.BlockSpec((tr * ns, LANES), lambda i, *_: (i, 0))],
            out_specs=pl.BlockSpec(memory_space=pl.ANY),
            scratch_shapes=[pltpu.VMEM((MOE_BLOCK * ns, LANES), U32), pltpu.SemaphoreType.DMA],
        ),
        out_shape=jax.ShapeDtypeStruct((n_rows * ns, LANES), U32),
        compiler_params=_cparams(1),
        name="dispatch",
    )(pstart, sizes, dest1, dest2, hn3_slab)


BLOCKS_PER_STEP = 4


def _expert_kernel(be_ref, nu_ref, x_ref, *refs):
    del be_ref
    w_refs, y_ref = refs[:-1], refs[-1]
    rows = y_ref.shape[0] // BLOCKS_PER_STEP
    n_live = nu_ref[0] - pl.program_id(0) * BLOCKS_PER_STEP

    def mlp(j):
        wg_ref, wu_ref, wd_ref = w_refs[3 * j:3 * j + 3]
        x = _from_slab(x_ref.at[pl.ds(j * rows, rows)], MOE_BLOCK).astype(BF16)
        hmid = _silu(_dot(x, wg_ref[...])) * _dot(x, wu_ref[...])
        return _dot(hmid, wd_ref[...])

    for live in range(BLOCKS_PER_STEP + 1):
        if live == 0:
            cond = n_live <= 0
        elif live == BLOCKS_PER_STEP:
            cond = n_live >= live
        else:
            cond = n_live == live

        @pl.when(cond)
        def _(live=live):
            ys = [mlp(j) for j in range(live)]
            for j in range(BLOCKS_PER_STEP):
                out = y_ref.at[pl.ds(j * rows, rows)]
                if j < live:
                    _to_slab(out, ys[j])
                else:
                    out[...] = jnp.zeros_like(out)


def _experts(blk_e, n_used, x_pad, wg, wu, wd):
    d, ff = wg.shape[-2:]
    blk = MOE_BLOCK * _slab_rows_per_token(d)
    n_steps = x_pad.shape[0] // (blk * BLOCKS_PER_STEP)
    assert n_steps * blk * BLOCKS_PER_STEP == x_pad.shape[0]
    rows = lambda i, be, nu: (jnp.minimum(i, (nu[0] - 1) // BLOCKS_PER_STEP), 0)
    weights = lambda j, r, c: pl.BlockSpec((None, r, c), lambda i, be, nu: (be[i * BLOCKS_PER_STEP + j], 0, 0))
    w_specs, w_args = [], []
    for j in range(BLOCKS_PER_STEP):
        w_specs += [weights(j, d, ff), weights(j, d, ff), weights(j, ff, d)]
        w_args += [wg, wu, wd]
    return pl.pallas_call(
        _expert_kernel,
        grid_spec=pltpu.PrefetchScalarGridSpec(
            num_scalar_prefetch=2,
            grid=(n_steps,),
            in_specs=[pl.BlockSpec((blk * BLOCKS_PER_STEP, LANES), rows)] + w_specs,
            out_specs=pl.BlockSpec((blk * BLOCKS_PER_STEP, LANES), lambda i, be, nu: (i, 0)),
        ),
        out_shape=jax.ShapeDtypeStruct(x_pad.shape, U32),
        compiler_params=_cparams(1),
        name="experts",
    )(blk_e, n_used, x_pad, *w_args)


def _combine_kernel(d1_ref, d2_ref, h2_ref, wt_ref, fnw_ref, y_ref, out_ref, b1_ref, b2_ref, sem, *, tr, ns):
    step = pl.program_id(0)
    n_steps = pl.num_programs(0)
    n_slots = GATHER_AHEAD + 1

    def copy(d, buf, slot, i):
        return pltpu.make_async_copy(_slab_rows(y_ref, d, ns), _slab_rows(buf.at[slot], i, ns), sem.at[slot])

    def issue_step(st):
        slot = st % n_slots

        def issue(g, carry):
            for u in range(ISSUE_UNROLL):
                i = g * ISSUE_UNROLL + u
                copy(d1_ref[st * tr + i], b1_ref, slot, i).start(priority=0)
                copy(d2_ref[st * tr + i], b2_ref, slot, i).start(priority=1)
            return carry

        lax.fori_loop(0, tr // ISSUE_UNROLL, issue, 0)

    for ahead in range(GATHER_AHEAD):
        @pl.when((step == 0) & (ahead < n_steps))
        def _(ahead=ahead):
            issue_step(ahead)

    @pl.when(step + GATHER_AHEAD < n_steps)
    def _():
        issue_step(step + GATHER_AHEAD)

    slot = step % n_slots

    def drain(g, carry):
        for _ in range(ISSUE_UNROLL):
            copy(0, b1_ref, slot, 0).wait()
            copy(0, b2_ref, slot, 0).wait()
        return carry

    lax.fori_loop(0, tr // ISSUE_UNROLL, drain, 0)

    wt = wt_ref[...].T
    h3 = (h2_ref[...] + wt[:, 0:1] * _from_slab(b1_ref.at[slot], tr)
          + wt[:, 1:2] * _from_slab(b2_ref.at[slot], tr))
    out_ref[...] = _rms(h3) * fnw_ref[...]


def _combine(dest1, dest2, h2, wts, fnw, y_pad, tr):
    t, d = h2.shape
    ns = _slab_rows_per_token(d)
    return pl.pallas_call(
        functools.partial(_combine_kernel, tr=tr, ns=ns),
        grid_spec=pltpu.PrefetchScalarGridSpec(
            num_scalar_prefetch=2,
            grid=(t // tr,),
            in_specs=[pl.BlockSpec((tr, d), lambda i, *_: (i, 0)),
                      pl.BlockSpec((8, tr), lambda i, *_: (0, i)),
                      pl.BlockSpec((1, d), lambda i, *_: (0, 0)),
                      pl.BlockSpec(memory_space=pl.ANY)],
            out_specs=pl.BlockSpec((tr, d), lambda i, *_: (i, 0)),
            scratch_shapes=[pltpu.VMEM((GATHER_AHEAD + 1, tr * ns, LANES), U32),
                            pltpu.VMEM((GATHER_AHEAD + 1, tr * ns, LANES), U32),
                            pltpu.SemaphoreType.DMA((GATHER_AHEAD + 1,))],
        ),
        out_shape=jax.ShapeDtypeStruct((t, d), F32),
        compiler_params=_cparams(1),
        name="combine",
    )(dest1, dest2, h2, wts, fnw, y_pad)


def _pick(n, pref):
    while n % pref:
        pref //= 2
    return pref


def kernel(x, mem, norm_mix_w, w_in, conv_w, gdn_a_log, gdn_dt_bias, gdn_out_norm_w, hgrn_lb, hgrn_out_norm_w, w_branch_a, w_branch_b, w_out, norm_xattn_w, norm_mem_w, xattn_wq, xattn_wkv, xattn_wo, norm_ffn_w, router_group_w, router_group_b, router_expert_w, router_expert_b, expert_w_gate, expert_w_up, expert_w_down, final_norm_w):
    b, s, d = x.shape
    t = b * s
    depth = w_in.shape[0]
    w = HEADS * HEAD_DIM
    qkv_w = 3 * w
    assert d == w and s % CHUNK == 0

    tc = _pick(s, 256)
    tm_proj = _pick(t, 1024)
    tm_post = _pick(s, 512)
    tr = _pick(t, 256)

    def pad_lanes(v, offset=0):
        return jnp.zeros((1, LANES), F32).at[0, offset:offset + v.shape[0]].set(v.astype(F32))

    assert depth == 1
    h3d = x
    for layer in range(depth):
        w_main_t, w_small_t = _wprep(w_in[layer].T, qkv_w, 2 * HEADS, 1024)
        p_main, p_small = _proj(h3d.reshape(t, d), norm_mix_w[layer][None, :], w_main_t, w_small_t, tm_proj, 2048)
        p3 = p_main.reshape(b, s, -1)
        o_a = _gdn(p3, p_small.reshape(b, s, LANES), conv_w[layer], pad_lanes(gdn_a_log[layer]),
                   pad_lanes(gdn_dt_bias[layer]), tc)
        o_b, (wg, wu, wd) = _hgrn(p3, hgrn_lb[layer:],
                                  (expert_w_gate[layer], expert_w_up[layer], expert_w_down[layer]), tc)
        kv = _kv(mem, norm_mem_w[layer][None, :], xattn_wkv[layer].astype(BF16))
        w_router = jnp.zeros((d, LANES), F32)
        w_router = w_router.at[:, :N_GROUPS].set(router_group_w[layer])
        w_router = w_router.at[:, N_GROUPS:N_GROUPS + N_EXPERTS].set(router_expert_w[layer])
        b_router = pad_lanes(router_group_b[layer]) + pad_lanes(router_expert_b[layer], N_GROUPS)
        tile8 = lambda v: jnp.tile(v.astype(F32), HEADS)[None, :]
        h2, hn3, logits_t = _post(
            h3d, o_a, o_b, p3, kv, tile8(gdn_out_norm_w[layer]), tile8(hgrn_out_norm_w[layer]),
            w_branch_a[layer].astype(BF16), w_branch_b[layer].astype(BF16), w_out[layer].astype(BF16),
            norm_xattn_w[layer][None, :], xattn_wq[layer].astype(BF16), xattn_wo[layer].astype(BF16),
            norm_ffn_w[layer][None, :], w_router, b_router, tm_post)

        m = t * 2
        n_rows = ((m + MOE_BLOCK - 1) // MOE_BLOCK) * MOE_BLOCK + N_EXPERTS * MOE_BLOCK
        n_blocks = -(-(n_rows // MOE_BLOCK) // BLOCKS_PER_STEP) * BLOCKS_PER_STEP
        n_rows = n_blocks * MOE_BLOCK
        dest, wts, blk_tab, exp_tab = _route(logits_t, _pick(t, 1024), n_blocks)
        dest1, dest2 = dest[0], dest[1]
        blk_e, n_used = blk_tab[0, :n_blocks], blk_tab[1, :1]
        sizes = exp_tab[0, N_GROUPS:N_GROUPS + N_EXPERTS]
        pstart = exp_tab[1, N_GROUPS:N_GROUPS + N_EXPERTS]

        x_pad = _dispatch(pstart, sizes, dest1, dest2, hn3, n_rows, _pick(t, 1024), _slab_rows_per_token(d))
        y_pad = _experts(blk_e, n_used, x_pad, wg, wu, wd)
        out = _combine(dest1, dest2, h2.reshape(t, d), wts, final_norm_w[None, :], y_pad, tr)
        h3d = out.reshape(b, s, d)
    return h3d
```
